```python
import jax, jax.numpy as jnp
from jax import lax
import numpy as np

D_MODEL = 1024
BATCH = 8
SEQ = 4096
DEPTH = 4

N_EVEN = (DEPTH + 1) // 2
N_ODD = DEPTH // 2
NORM_EPS = 1e-6
D_FF = 2816

CONV_WIDTH = D_MODEL // 2
CONV_GROUPS = 8
CONV_TAPS = 3
RWKV_WIDTH = D_MODEL - CONV_WIDTH
RWKV_HEAD = 64
RWKV_HEADS = RWKV_WIDTH // RWKV_HEAD
DECAY_RANK = 32
ICLR_RANK = 32
GATE_RANK = 96
RWKV_GN_EPS = 64e-5
RWKV_SHIFT_COLS = 3 * RWKV_WIDTH + DECAY_RANK + ICLR_RANK + GATE_RANK
EVEN_IN = 3 * CONV_WIDTH + RWKV_SHIFT_COLS
_RWKV_SPLITS = (RWKV_WIDTH, 2 * RWKV_WIDTH, 3 * RWKV_WIDTH,
                3 * RWKV_WIDTH + DECAY_RANK, 3 * RWKV_WIDTH + DECAY_RANK + ICLR_RANK)

MLA_HEADS = 8
Q_RANK = 384
KV_RANK = 256
NOPE_DIM = 128
ROPE_DIM = 64
V_DIM = 128
QK_DIM = NOPE_DIM + ROPE_DIM
ODD_IN = Q_RANK + KV_RANK + ROPE_DIM
ROPE_THETA = 10000.0
Q_BLOCK = 128
ATTN_SCALE = QK_DIM ** -0.5

kernel_name = "hybrid_conv_rwkv7_mla_macaron"


def _rmsnorm(x, gain, eps=NORM_EPS):
    xf = x.astype(jnp.float32)
    y = xf * lax.rsqrt(jnp.mean(xf * xf, axis=-1, keepdims=True) + eps)
    return (y * gain.astype(jnp.float32)).astype(x.dtype)


def _swiglu(h, w_gate, w_up, w_down):
    return (jax.nn.silu(h @ w_gate) * (h @ w_up)) @ w_down


def _shift(u):
    return jnp.pad(u, ((0, 0), (1, 0), (0, 0)))[:, :-1]


def _short_conv(u, w):
    T = u.shape[1]
    up = jnp.pad(u, ((0, 0), (CONV_TAPS - 1, 0), (0, 0)))
    return sum(up[:, j:j + T] * w[j] for j in range(CONV_TAPS))


def _rwkv7_scan(r, decay, k, v, a_vec, b_vec):
    B, T, H, N = r.shape

    def step(S, inp):
        r_t, w_t, k_t, v_t, a_t, b_t = inp
        Sa = jnp.einsum('bhvk,bhk->bhv', S, a_t)
        S = S * w_t[:, :, None, :] + Sa[..., None] * b_t[:, :, None, :] + v_t[..., None] * k_t[:, :, None, :]
        return S, jnp.einsum('bhvk,bhk->bhv', S, r_t)

    xs = tuple(jnp.moveaxis(t, 1, 0) for t in (r, decay, k, v, a_vec, b_vec))
    S0 = jnp.zeros((B, H, N, N), jnp.float32)
    _, y = lax.scan(step, S0, xs)
    return jnp.moveaxis(y, 0, 1)


def _conv_rwkv_mixer(h, w_in, conv_w, mu_shift, w0, w_up, a0, a_up, g_up,
                     k_k, k_a, r_k, ln_w, ln_b, w_out):
    B, T, _ = h.shape
    f32 = jnp.float32
    p = h @ w_in
    p_conv, p_rwkv = p[..., :3 * CONV_WIDTH], p[..., 3 * CONV_WIDTH:]
    gate_b, gate_c, h_in = jnp.split(p_conv, 3, axis=-1)
    y_conv = gate_b * _short_conv(gate_c * h_in, conv_w)
    p_rwkv = p_rwkv + (_shift(p_rwkv) - p_rwkv) * mu_shift
    r, k, v, dw, da, dg = jnp.split(p_rwkv, _RWKV_SPLITS, axis=-1)
    log_w = -jax.nn.softplus(-(w0 + jnp.tanh(dw) @ w_up).astype(f32)) - 0.5
    decay = jnp.exp(-jnp.exp(log_w))
    a = jax.nn.sigmoid((a0 + da @ a_up).astype(f32))
    g = jax.nn.sigmoid(dg) @ g_up

    def heads(t):
        return t.astype(f32).reshape(B, T, RWKV_HEADS, RWKV_HEAD)

    r_h, v_h, a_h, decay_h = heads(r), heads(v), heads(a), heads(decay)
    kk = heads(k * k_k)
    kk = kk * lax.rsqrt(jnp.maximum(jnp.sum(kk * kk, axis=-1, keepdims=True), 1e-24))
    k_h = heads(k) * (1.0 + (a_h - 1.0) * k_a.astype(f32).reshape(RWKV_HEADS, RWKV_HEAD))
    y = _rwkv7_scan(r_h, decay_h, k_h, v_h, -kk, kk * a_h)
    mu = jnp.mean(y, axis=-1, keepdims=True)
    var = jnp.mean(jnp.square(y - mu), axis=-1, keepdims=True)
    y = (y - mu) * lax.rsqrt(var + RWKV_GN_EPS) * ln_w.astype(f32).reshape(RWKV_HEADS, RWKV_HEAD) \
        + ln_b.astype(f32).reshape(RWKV_HEADS, RWKV_HEAD)
    y = y + jnp.sum(r_h * k_h * r_k.astype(f32), axis=-1, keepdims=True) * v_h
    y_rwkv = y.reshape(B, T, RWKV_WIDTH).astype(h.dtype) * g
    return jnp.concatenate([y_conv, y_rwkv], axis=-1) @ w_out


def _rope(t, cos, sin):
    t1, t2 = t[..., :ROPE_DIM // 2], t[..., ROPE_DIM // 2:]
    return jnp.concatenate([t1 * cos - t2 * sin, t2 * cos + t1 * sin], axis=-1)


def _causal_attention(q, k, v):
    B, T, H, Dk = q.shape
    nb = T // Q_BLOCK
    qb = q.reshape(B, nb, Q_BLOCK, H, Dk).transpose(1, 0, 3, 2, 4)
    kpos = jnp.arange(T)

    def one_block(args):
        qi, i = args
        s = jnp.einsum('bhqd,bkhd->bhqk', qi, k, preferred_element_type=jnp.float32) * ATTN_SCALE
        qpos = i * Q_BLOCK + jnp.arange(Q_BLOCK)
        s = jnp.where(kpos[None, :] <= qpos[:, None], s, jnp.finfo(jnp.float32).min)
        p = jax.nn.softmax(s, axis=-1)
        return jnp.einsum('bhqk,bkhd->bqhd', p.astype(v.dtype), v)

    out = lax.map(one_block, (qb, jnp.arange(nb)))
    return out.transpose(1, 0, 2, 3, 4).reshape(B, T, H, v.shape[-1])


def _mla_mixer(h, cos, sin, w_in, q_a_norm, kv_a_norm, w_q_up, w_kv_up, q_norm, k_norm, w_out):
    B, T, _ = h.shape
    p = h @ w_in
    c_q, c_kv, k_pe = jnp.split(p, (Q_RANK, Q_RANK + KV_RANK), axis=-1)
    q = (_rmsnorm(c_q, q_a_norm) @ w_q_up).reshape(B, T, MLA_HEADS, QK_DIM)
    kv = (_rmsnorm(c_kv, kv_a_norm) @ w_kv_up).reshape(B, T, MLA_HEADS, NOPE_DIM + V_DIM)
    k_nope, v = kv[..., :NOPE_DIM], kv[..., NOPE_DIM:]
    k = jnp.concatenate([k_nope, jnp.broadcast_to(k_pe[:, :, None, :], (B, T, MLA_HEADS, ROPE_DIM))], axis=-1)
    q = _rmsnorm(q, q_norm)
    k = _rmsnorm(k, k_norm)
    q = jnp.concatenate([q[..., :NOPE_DIM], _rope(q[..., NOPE_DIM:], cos, sin)], axis=-1)
    k = jnp.concatenate([k[..., :NOPE_DIM], _rope(k[..., NOPE_DIM:], cos, sin)], axis=-1)
    o = _causal_attention(q, k, v)
    return o.reshape(B, T, MLA_HEADS * V_DIM) @ w_out


def setup_inputs(seed: int = 0) -> dict:
    key = jax.random.key(seed)
    ks = iter(jax.random.split(key, 32))
    f32 = jnp.float32

    def nrm(shape, fan_in):
        return jax.random.normal(next(ks), shape, f32) * fan_in ** -0.5

    def gain(shape):
        return 1.0 + 0.02 * jax.random.normal(next(ks), shape, f32)

    def unif(shape, lo, hi):
        return jax.random.uniform(next(ks), shape, f32, lo, hi)

    x = jax.random.normal(next(ks), (BATCH, SEQ, D_MODEL), f32)
    offset = jax.random.randint(next(ks), (BATCH, 1), 0, 1024, jnp.int32)
    positions = (offset + jnp.arange(SEQ, dtype=jnp.int32)[None, :]).astype(jnp.int32)
    return {
        "x": x,
        "positions": positions,
        "norm_gains": gain((DEPTH, 3, D_MODEL)),
        "ffn_w_gate": nrm((DEPTH, 2, D_MODEL, D_FF), D_MODEL),
        "ffn_w_up": nrm((DEPTH, 2, D_MODEL, D_FF), D_MODEL),
        "ffn_w_down": nrm((DEPTH, 2, D_FF, D_MODEL), D_FF),
        "even_w_in": nrm((N_EVEN, D_MODEL, EVEN_IN), D_MODEL),
        "even_conv_w": nrm((N_EVEN, CONV_TAPS, CONV_WIDTH), CONV_TAPS),
        "even_mu_shift": unif((N_EVEN, RWKV_SHIFT_COLS), 0.0, 1.0),
        "rwkv_w0": unif((N_EVEN, RWKV_WIDTH), -6.0, 0.0),
        "rwkv_w_up": nrm((N_EVEN, DECAY_RANK, RWKV_WIDTH), DECAY_RANK),
        "rwkv_a0": 0.1 * jax.random.normal(next(ks), (N_EVEN, RWKV_WIDTH), f32),
        "rwkv_a_up": nrm((N_EVEN, ICLR_RANK, RWKV_WIDTH), ICLR_RANK),
        "rwkv_g_up": nrm((N_EVEN, GATE_RANK, RWKV_WIDTH), GATE_RANK),
        "rwkv_k_k": 0.85 + 0.05 * jax.random.normal(next(ks), (N_EVEN, RWKV_WIDTH), f32),
        "rwkv_k_a": gain((N_EVEN, RWKV_WIDTH)),
        "rwkv_r_k": 0.1 * jax.random.normal(next(ks), (N_EVEN, RWKV_HEADS, RWKV_HEAD), f32),
        "rwkv_ln_w": gain((N_EVEN, RWKV_WIDTH)),
        "rwkv_ln_b": 0.02 * jax.random.normal(next(ks), (N_EVEN, RWKV_WIDTH), f32),
        "even_w_out": nrm((N_EVEN, CONV_WIDTH + RWKV_WIDTH, D_MODEL), CONV_WIDTH + RWKV_WIDTH),
        "odd_w_in": nrm((N_ODD, D_MODEL, ODD_IN), D_MODEL),
        "mla_q_a_norm": gain((N_ODD, Q_RANK)),
        "mla_kv_a_norm": gain((N_ODD, KV_RANK)),
        "mla_w_q_up": nrm((N_ODD, Q_RANK, MLA_HEADS * QK_DIM), Q_RANK),
        "mla_w_kv_up": nrm((N_ODD, KV_RANK, MLA_HEADS * (NOPE_DIM + V_DIM)), KV_RANK),
        "mla_q_norm": gain((N_ODD, QK_DIM)),
        "mla_k_norm": gain((N_ODD, QK_DIM)),
        "odd_w_out": nrm((N_ODD, MLA_HEADS * V_DIM, D_MODEL), MLA_HEADS * V_DIM),
    }


def reference(x, positions, norm_gains, ffn_w_gate, ffn_w_up, ffn_w_down,
              even_w_in, even_conv_w, even_mu_shift, rwkv_w0, rwkv_w_up, rwkv_a0, rwkv_a_up,
              rwkv_g_up, rwkv_k_k, rwkv_k_a, rwkv_r_k, rwkv_ln_w, rwkv_ln_b, even_w_out,
              odd_w_in, mla_q_a_norm, mla_kv_a_norm, mla_w_q_up, mla_w_kv_up,
              mla_q_norm, mla_k_norm, odd_w_out):
    inv_freq = ROPE_THETA ** (-jnp.arange(0, ROPE_DIM, 2, dtype=jnp.float32) / ROPE_DIM)
    ang = positions.astype(jnp.float32)[..., None] * inv_freq
    cos = jnp.cos(ang)[:, :, None, :].astype(x.dtype)
    sin = jnp.sin(ang)[:, :, None, :].astype(x.dtype)

    for layer in range(DEPTH):
        g = norm_gains[layer]
        x = x + 0.5 * _swiglu(_rmsnorm(x, g[0]), ffn_w_gate[layer, 0], ffn_w_up[layer, 0], ffn_w_down[layer, 0])
        h = _rmsnorm(x, g[1])
        if layer % 2 == 0:
            i = layer // 2
            mix = _conv_rwkv_mixer(h, even_w_in[i], even_conv_w[i], even_mu_shift[i], rwkv_w0[i],
                                   rwkv_w_up[i], rwkv_a0[i], rwkv_a_up[i], rwkv_g_up[i], rwkv_k_k[i],
                                   rwkv_k_a[i], rwkv_r_k[i], rwkv_ln_w[i], rwkv_ln_b[i], even_w_out[i])
        else:
            j = layer // 2
            mix = _mla_mixer(h, cos, sin, odd_w_in[j], mla_q_a_norm[j], mla_kv_a_norm[j], mla_w_q_up[j],
                             mla_w_kv_up[j], mla_q_norm[j], mla_k_norm[j], odd_w_out[j])
        x = x + mix.astype(x.dtype)
        x = x + 0.5 * _swiglu(_rmsnorm(x, g[2]), ffn_w_gate[layer, 1], ffn_w_up[layer, 1], ffn_w_down[layer, 1])
    return x
```

```python
import functools
import math

import numpy as np
import jax
import jax.numpy as jnp
from jax import lax
from jax.experimental import pallas as pl
from jax.experimental.pallas import tpu as pltpu

F32 = jnp.float32
BF16 = jnp.bfloat16

NORM_EPS = 1e-6
RWKV_GN_EPS = 64e-5
ROPE_THETA = 10000.0
KK_NORM_FLOOR = 1e-24

LANES = 128
HALO = 16
VMEM_LIMIT = 56 * 1024 * 1024
RWKV_CHUNK = 64
TOKEN_TILE = 512
ATTN_TILE = 512

_HIGHEST = lax.Precision.HIGHEST


def _dot(a, b):
    return jnp.dot(a, b, preferred_element_type=F32)


def _dot_bf(a, b):
    return jnp.dot(a.astype(BF16), b.astype(BF16), preferred_element_type=F32)


def _dot_hp(a, b):
    return jnp.dot(a, b, preferred_element_type=F32, precision=_HIGHEST)


def _dot_nt(a, b):
    return lax.dot_general(a, b, (((1,), (1,)), ((), ())), preferred_element_type=F32)


def _dot_tn(a, b):
    return lax.dot_general(a, b, (((0,), (0,)), ((), ())), preferred_element_type=F32)


def _rms(x, gain, eps=NORM_EPS):
    return x * lax.rsqrt(jnp.mean(x * x, axis=-1, keepdims=True) + eps) * gain


def _resident(shape):
    nd = len(shape)
    return pl.BlockSpec(shape, lambda *_: (0,) * nd, pipeline_mode=pl.Buffered(1))


def _params(*sem):
    return pltpu.CompilerParams(dimension_semantics=sem, vmem_limit_bytes=VMEM_LIMIT)


def _tile(n, want):
    t = min(n, want)
    assert n % t == 0, (n, t)
    return t


def _ffn_kernel(x_ref, gain_ref, wg_ref, wu_ref, wd_ref, o_ref, act_ref, *, tf):
    x = x_ref[...]
    h = _rms(x, gain_ref[...]).astype(BF16)
    for f in range(wg_ref.shape[1] // tf):
        cols = slice(f * tf, (f + 1) * tf)
        g = _dot(h, wg_ref[:, cols])
        u = _dot(h, wu_ref[:, cols])
        act_ref[:, cols] = (g * jax.nn.sigmoid(g) * u).astype(BF16)
    o_ref[...] = x + 0.5 * _dot(act_ref[...], wd_ref[...])


def _ffn(x2d, gain, wg, wu, wd):
    m, d = x2d.shape
    f = wg.shape[1]
    tm = _tile(m, TOKEN_TILE)
    tf = 256 if f % 256 == 0 else LANES
    return pl.pallas_call(
        functools.partial(_ffn_kernel, tf=tf),
        out_shape=jax.ShapeDtypeStruct((m, d), F32),
        grid=(m // tm,),
        in_specs=[pl.BlockSpec((tm, d), lambda i: (i, 0)),
                  _resident((1, d)), _resident((d, f)), _resident((d, f)), _resident((f, d))],
        out_specs=pl.BlockSpec((tm, d), lambda i: (i, 0)),
        scratch_shapes=[pltpu.VMEM((tm, f), BF16)],
        compiler_params=_params("parallel"),
        name="ffn",
    )(x2d, gain.reshape(1, d), wg, wu, wd)


def _even_in_kernel(x_ref, xh_ref, gain_ref, w_ref, mu_ref, convw_ref, w0_ref, wup_ref, a0_ref, aup_ref,
                    gup_ref, kk_ref, ka_ref, rk_ref, hsum_ref,
                    lw_o, r_o, k_o, v_o, a_o, b_o, g_o, yc_o, bonus_o,
                    h_s, p_s, *, cw, rw):
    tm = x_ref.shape[1]
    gain = gain_ref[...]
    xh = jnp.where(pl.program_id(1) > 0, xh_ref[0], 0.0)
    h_s[0:HALO, :] = _rms(xh, gain).astype(BF16)
    h_s[HALO:, :] = _rms(x_ref[0], gain).astype(BF16)

    def proj(c0, n):
        return _dot(h_s[...], w_ref[:, c0:c0 + n])

    def staged(back, n):
        return p_s[pl.ds(HALO - back, tm), 0:n]

    gate_b = proj(0, cw)[HALO:]
    p_s[:, 0:cw] = proj(cw, cw) * proj(2 * cw, cw)
    cwt = convw_ref[...]
    yc_o[0] = gate_b * (staged(2, cw) * cwt[0:1] + staged(1, cw) * cwt[1:2] + staged(0, cw) * cwt[2:3])

    def shifted(c0, n):
        p_s[:, 0:n] = proj(c0, n)
        cur = staged(0, n)
        return cur + (staged(1, n) - cur) * mu_ref[:, c0 - 3 * cw:c0 - 3 * cw + n]

    base = 3 * cw
    r = shifted(base, rw)
    k = shifted(base + rw, rw)
    v = shifted(base + 2 * rw, rw)
    lora = shifted(base + 3 * rw, 3 * LANES)
    dw, da, dg = lora[:, 0:LANES], lora[:, LANES:2 * LANES], lora[:, 2 * LANES:3 * LANES]

    zw = w0_ref[...] + _dot_hp(jnp.tanh(dw), wup_ref[...])
    lw_o[0] = -math.exp(-0.5) * jax.nn.sigmoid(zw)
    iclr = jax.nn.sigmoid(a0_ref[...] + _dot_hp(da, aup_ref[...]))
    g_o[0] = _dot_hp(jax.nn.sigmoid(dg), gup_ref[...])
    hsum = hsum_ref[...]
    kkr = k * kk_ref[...]
    kk = kkr * lax.rsqrt(jnp.maximum(_dot_hp(kkr * kkr, hsum), KK_NORM_FLOOR))
    kh = k * (1.0 + (iclr - 1.0) * ka_ref[...])
    r_o[0] = r
    k_o[0] = kh
    v_o[0] = v
    a_o[0] = -kk
    b_o[0] = kk * iclr
    bonus_o[0] = _dot_hp(r * kh * rk_ref[...], hsum) * v


def _rwkv_chunk_kernel(lw_ref, r_ref, k_ref, v_ref, a_ref, b_ref, y_ref, st_ref, *, chunk):
    @pl.when(pl.program_id(1) == 0)
    def _():
        st_ref[...] = jnp.zeros_like(st_ref)

    tc, width = lw_ref.shape[1], lw_ref.shape[2]
    c2 = 2 * chunk
    assert c2 == LANES
    row = lax.broadcasted_iota(jnp.int32, (c2, c2), 0)
    col = lax.broadcasted_iota(jnp.int32, (c2, c2), 1)
    same = (row // chunk) == (col // chunk)
    strict = same & (col < row)
    incl = same & (col <= row)
    eye = row == col
    eye_f = eye.astype(F32)
    tri = (lax.broadcasted_iota(jnp.int32, (chunk, chunk), 1)
           <= lax.broadcasted_iota(jnp.int32, (chunk, chunk), 0)).astype(F32)
    head0 = lax.broadcasted_iota(jnp.int32, (chunk, LANES), 1) < chunk

    def stack(x):
        return jnp.concatenate([jnp.where(head0, x, 0.0), jnp.where(head0, 0.0, x)], axis=0)

    for c in range(tc // chunk):
        rows = slice(c * chunk, (c + 1) * chunk)
        for p in range(width // LANES):
            lanes = slice(p * LANES, (p + 1) * LANES)
            lw = lw_ref[0, rows, lanes]
            cum = _dot_hp(tri, lw)
            w_in = jnp.exp(cum)
            w_out = jnp.exp(-cum)
            w_end = w_in[chunk - 1:chunk, :]
            a_s = stack(a_ref[0, rows, lanes] * jnp.exp(cum - lw))
            r_s = stack(r_ref[0, rows, lanes] * w_in)
            b_s = stack(b_ref[0, rows, lanes] * w_out)
            k_s = stack(k_ref[0, rows, lanes] * w_out)
            v_s = stack(v_ref[0, rows, lanes]).astype(BF16)
            ar = jnp.concatenate([a_s, r_s], axis=0).astype(BF16)
            bk = jnp.concatenate([b_s, k_s], axis=0).astype(BF16)
            att = _dot_nt(ar, bk)
            a_ab = jnp.where(strict, att[:c2, :c2], 0.0)
            a_ak = jnp.where(strict, att[:c2, c2:], 0.0)
            a_rb = jnp.where(incl, att[c2:, :c2], 0.0).astype(BF16)
            a_rk = jnp.where(incl, att[c2:, c2:], 0.0).astype(BF16)
            power = a_ab
            tinv = eye_f + a_ab
            for _ in range(int(math.log2(chunk)) - 1):
                power = _dot_bf(power, power)
                tinv = tinv + _dot_bf(tinv, power)
            pq = _dot_bf(tinv, jnp.concatenate([a_s, _dot_bf(a_ak, v_s)], axis=1)).astype(BF16)
            gy = _dot(a_rb, pq)
            g_mat = r_s + gy[:, :c2]
            y0 = gy[:, c2:] + _dot(a_rk, v_s)
            mn = _dot_tn((b_s * w_end).astype(BF16), pq)
            m_mat = jnp.where(eye, w_end, 0.0) + mn[:, :c2]
            n0 = mn[:, c2:] + _dot_tn((k_s * w_end).astype(BF16), v_s)
            state = st_ref[p].astype(BF16)
            ys = _dot_bf(g_mat, state) + y0
            st_ref[p] = _dot_bf(m_mat, state) + n0
            y_ref[0, rows, lanes] = ys[:chunk] + ys[chunk:]


def _even_out_kernel(x_ref, y_ref, bonus_ref, g_ref, yc_ref, lnw_ref, lnb_ref, hmean_ref, wo_ref, o_ref, *, cw):
    y = y_ref[...]
    hmean = hmean_ref[...]
    d = y - _dot_hp(y, hmean)
    var = _dot_hp(d * d, hmean)
    yn = d * lax.rsqrt(var + RWKV_GN_EPS) * lnw_ref[...] + lnb_ref[...]
    y_rwkv = (yn + bonus_ref[...]) * g_ref[...]
    mix = _dot(yc_ref[...].astype(BF16), wo_ref[0:cw, :]) + _dot(y_rwkv.astype(BF16), wo_ref[cw:, :])
    o_ref[...] = x_ref[...] + mix


def _conv_rwkv_mixer(x, gain, w_in, conv_w, mu_shift, w0, w_up, a0, a_up, g_up, k_k, k_a, r_k, ln_w, ln_b, w_out):
    b, t, d = x.shape
    cw = conv_w.shape[1]
    heads, hd = r_k.shape
    rw = heads * hd
    ranks = (w_up.shape[0], a_up.shape[0], g_up.shape[0])
    assert max(ranks) <= LANES and rw % LANES == 0 and cw == rw and LANES % hd == 0

    base = 3 * cw + 3 * rw
    offs = (base, base + ranks[0], base + ranks[0] + ranks[1])

    def padded_cols(arr):
        blocks = [jnp.pad(arr[..., o:o + n], [(0, 0)] * (arr.ndim - 1) + [(0, LANES - n)])
                  for o, n in zip(offs, ranks)]
        return jnp.concatenate([arr[..., :base]] + blocks, axis=-1)

    w_cat = padded_cols(w_in).astype(BF16)
    mu_cat = padded_cols(jnp.pad(mu_shift.reshape(1, -1), ((0, 0), (3 * cw, 0))))[:, 3 * cw:]
    pad_rows = lambda m: jnp.pad(m, ((0, LANES - m.shape[0]), (0, 0)))
    head_id = np.arange(rw) // hd
    hsum = jnp.asarray((head_id[:, None] == head_id[None, :]).astype(np.float32))
    row = lambda vec: vec.reshape(1, -1)

    tm = _tile(t, TOKEN_TILE)
    nt = t // tm
    wide = w_cat.shape[1]
    tok = pl.BlockSpec((1, tm, rw), lambda bi, i: (bi, i, 0))
    seq = jax.ShapeDtypeStruct((b, t, rw), F32)
    lw, r, kh, v, a_vec, b_vec, gate, y_conv, bonus = pl.pallas_call(
        functools.partial(_even_in_kernel, cw=cw, rw=rw),
        out_shape=[seq] * 9,
        grid=(b, nt),
        in_specs=[pl.BlockSpec((1, tm, d), lambda bi, i: (bi, i, 0)),
                  pl.BlockSpec((1, HALO, d), lambda bi, i: (bi, jnp.maximum(i * (tm // HALO) - 1, 0), 0)),
                  _resident((1, d)), _resident((d, wide)), _resident((1, wide - 3 * cw)),
                  _resident(conv_w.shape), _resident((1, rw)), _resident((LANES, rw)),
                  _resident((1, rw)), _resident((LANES, rw)), _resident((LANES, rw)),
                  _resident((1, rw)), _resident((1, rw)), _resident((1, rw)), _resident((rw, rw))],
        out_specs=[tok] * 9,
        scratch_shapes=[pltpu.VMEM((tm + HALO, d), BF16), pltpu.VMEM((tm + HALO, max(cw, rw)), F32)],
        compiler_params=_params("parallel", "parallel"),
        name="even_in",
    )(x, x, row(gain), w_cat, mu_cat, conv_w, row(w0), pad_rows(w_up), row(a0), pad_rows(a_up),
      pad_rows(g_up), row(k_k), row(k_a), row(r_k), hsum)

    tc = _tile(t, 2 * RWKV_CHUNK)
    blk = pl.BlockSpec((1, tc, rw), lambda bi, j: (bi, j, 0))
    y = pl.pallas_call(
        functools.partial(_rwkv_chunk_kernel, chunk=RWKV_CHUNK),
        out_shape=seq,
        grid=(b, t // tc),
        in_specs=[blk] * 6,
        out_specs=blk,
        scratch_shapes=[pltpu.VMEM((rw // LANES, LANES, LANES), F32)],
        compiler_params=_params("parallel", "arbitrary"),
        name="rwkv_chunk",
    )(lw, r, kh, v, a_vec, b_vec)

    m = b * t
    tmo = _tile(m, TOKEN_TILE)
    flat = lambda z: z.reshape(m, -1)
    tokf = pl.BlockSpec((tmo, rw), lambda i: (i, 0))
    out = pl.pallas_call(
        functools.partial(_even_out_kernel, cw=cw),
        out_shape=jax.ShapeDtypeStruct((m, d), F32),
        grid=(m // tmo,),
        in_specs=[pl.BlockSpec((tmo, d), lambda i: (i, 0)), tokf, tokf, tokf, tokf,
                  _resident((1, rw)), _resident((1, rw)), _resident((rw, rw)), _resident((cw + rw, d))],
        out_specs=pl.BlockSpec((tmo, d), lambda i: (i, 0)),
        compiler_params=_params("parallel"),
        name="even_out",
    )(flat(x), flat(y), flat(bonus), flat(gate), flat(y_conv), row(ln_w), row(ln_b), hsum / hd,
      w_out.astype(BF16))
    return out.reshape(b, t, d)


def _rope_table_kernel(pos_ref, freq_ref, cos_ref, sin_ref):
    ang = pos_ref[...].astype(F32) * freq_ref[...]
    cos_ref[...] = jnp.cos(ang)
    sin_ref[...] = jnp.sin(ang)


def _rope_tables(positions, rope_dim):
    b, t = positions.shape
    half = rope_dim // 2
    assert LANES % half == 0
    per_row = LANES // half
    rows = b * t // per_row
    inv_freq = ROPE_THETA ** (-jnp.arange(0, rope_dim, 2, dtype=F32) / rope_dim)
    pos_rep = jnp.repeat(positions.reshape(-1), half).reshape(rows, LANES)
    tr = _tile(rows, 1024)
    blk = pl.BlockSpec((tr, LANES), lambda i: (i, 0))
    cos, sin = pl.pallas_call(
        _rope_table_kernel,
        out_shape=[jax.ShapeDtypeStruct((rows, LANES), F32)] * 2,
        grid=(rows // tr,),
        in_specs=[blk, _resident((1, LANES))],
        out_specs=[blk, blk],
        compiler_params=_params("parallel"),
        name="rope_table",
    )(pos_rep, jnp.tile(inv_freq, per_row).reshape(1, LANES))
    cos = cos.reshape(b, t, half)
    sin = sin.reshape(b, t, half)
    return jnp.concatenate([cos, cos, sin, sin], axis=-1)


def _mla_in_kernel(x_ref, cs_ref, gain_ref, win_ref, qa_ref, kva_ref, wq_ref, wkv_ref,
                   gqn_ref, gqr_ref, gkn_ref, gkr_ref, q_o, k_o, v_o,
                   *, heads, q_rank, kv_rank, nope, rope, scale):
    qk_dim = nope + rope
    tm = x_ref.shape[1]
    h = _rms(x_ref[0], gain_ref[...]).astype(BF16)
    p = _dot(h, win_ref[...])
    c_q = _rms(p[:, :q_rank], qa_ref[...]).astype(BF16)
    c_kv = _rms(p[:, q_rank:q_rank + kv_rank], kva_ref[...]).astype(BF16)
    kpe = p[:, q_rank + kv_rank:]
    q = _dot(c_q, wq_ref[...])
    kv = _dot(c_kv, wkv_ref[...])
    cs = cs_ref[0]
    low = lax.broadcasted_iota(jnp.int32, (tm, 2 * rope), 1) < rope

    def rotary(blk):
        bc = blk * cs
        return jnp.where(low, bc + pltpu.roll(bc, rope, axis=1), 0.0)

    sumsq = lambda z: jnp.sum(z * z, axis=-1, keepdims=True)
    kpe_sq = 0.5 * sumsq(kpe)
    k_rot = rotary(kpe * gkr_ref[...])
    blk = nope + 2 * rope
    for hd in range(heads):
        qn = q[:, hd * blk:hd * blk + nope]
        qr = q[:, hd * blk + nope:(hd + 1) * blk]
        rs_q = lax.rsqrt((sumsq(qn) + 0.5 * sumsq(qr)) / qk_dim + NORM_EPS) * scale
        q_o[0, hd, :, 0:nope] = (qn * rs_q * gqn_ref[...]).astype(BF16)
        q_o[0, hd, :, nope:blk] = (rotary(qr * gqr_ref[...]) * rs_q).astype(BF16)
        kn = kv[:, hd * blk:hd * blk + nope]
        rs_k = lax.rsqrt((sumsq(kn) + kpe_sq) / qk_dim + NORM_EPS)
        k_o[0, hd, :, 0:nope] = (kn * rs_k * gkn_ref[...]).astype(BF16)
        k_o[0, hd, :, nope:blk] = (k_rot * rs_k).astype(BF16)
        v_o[0, hd] = kv[:, hd * blk + nope:(hd + 1) * blk].astype(BF16)


def _flash_kernel(q_ref, k_ref, v_ref, o_ref):
    i = pl.program_id(2)
    tq = q_ref.shape[2]
    q = q_ref[0, 0]

    def step(j, carry, masked):
        m, l, acc = carry
        start = pl.multiple_of(j * tq, tq)
        s = _dot_nt(q, k_ref[0, 0, pl.ds(start, tq), :])
        if masked:
            s = jnp.where(lax.broadcasted_iota(jnp.int32, s.shape, 1)
                          <= lax.broadcasted_iota(jnp.int32, s.shape, 0), s, -jnp.inf)
        m_new = jnp.maximum(m, jnp.max(s, axis=-1, keepdims=True))
        alpha = jnp.exp(m - m_new)
        pr = jnp.exp(s - m_new)
        l = alpha * l + jnp.sum(pr, axis=-1, keepdims=True)
        acc = alpha * acc + _dot(pr.astype(BF16), v_ref[0, 0, pl.ds(start, tq), :])
        return m_new, l, acc

    init = (jnp.full((tq, 1), -jnp.inf, F32), jnp.zeros((tq, 1), F32), jnp.zeros((tq, v_ref.shape[3]), F32))
    carry = lax.fori_loop(0, i, functools.partial(step, masked=False), init)
    _, l, acc = step(i, carry, True)
    o_ref[0] = (acc / l).astype(o_ref.dtype)


def _mla_out_kernel(x_ref, o_ref, wo_ref, out_ref):
    out_ref[...] = x_ref[...] + _dot(o_ref[...], wo_ref[...])


def _mla_mixer(x, cs, gain, w_in, q_a_norm, kv_a_norm, w_q_up, w_kv_up, q_norm, k_norm, w_out):
    b, t, d = x.shape
    q_rank, kv_rank, qk_dim = q_a_norm.shape[0], kv_a_norm.shape[0], q_norm.shape[0]
    rope = w_in.shape[1] - q_rank - kv_rank
    nope = qk_dim - rope
    heads = w_q_up.shape[1] // qk_dim
    v_dim = w_kv_up.shape[1] // heads - nope
    half = rope // 2
    blk = nope + 2 * rope
    assert nope == LANES and v_dim == LANES and 2 * rope == LANES and q_rank % LANES == 0 and kv_rank % LANES == 0

    def rot_cols(w):
        t1, t2 = w[..., :half], w[..., half:]
        return jnp.concatenate([t1, t2, -t2, t1], axis=-1)

    def rot_gain(g):
        g1, g2 = g[:half], g[half:]
        return jnp.concatenate([g1, g2, g2, g1]).reshape(1, 2 * rope)

    w_in_ext = jnp.concatenate([w_in[:, :q_rank + kv_rank], rot_cols(w_in[:, q_rank + kv_rank:])], axis=1).astype(BF16)
    wq = w_q_up.reshape(q_rank, heads, qk_dim)
    wq_ext = jnp.concatenate([wq[..., :nope], rot_cols(wq[..., nope:])], axis=-1).reshape(q_rank, heads * blk).astype(BF16)
    row = lambda vec: vec.reshape(1, -1)

    tm = _tile(t, TOKEN_TILE)
    head_blk = lambda width: pl.BlockSpec((1, heads, tm, width), lambda bi, i: (bi, 0, i, 0))
    q, k, v = pl.pallas_call(
        functools.partial(_mla_in_kernel, heads=heads, q_rank=q_rank, kv_rank=kv_rank, nope=nope, rope=rope,
                          scale=qk_dim ** -0.5),
        out_shape=[jax.ShapeDtypeStruct((b, heads, t, blk), BF16), jax.ShapeDtypeStruct((b, heads, t, blk), BF16),
                   jax.ShapeDtypeStruct((b, heads, t, v_dim), BF16)],
        grid=(b, t // tm),
        in_specs=[pl.BlockSpec((1, tm, d), lambda bi, i: (bi, i, 0)),
                  pl.BlockSpec((1, tm, 2 * rope), lambda bi, i: (bi, i, 0)),
                  _resident((1, d)), _resident(w_in_ext.shape), _resident((1, q_rank)), _resident((1, kv_rank)),
                  _resident(wq_ext.shape), _resident(w_kv_up.shape),
                  _resident((1, nope)), _resident((1, 2 * rope)), _resident((1, nope)), _resident((1, 2 * rope))],
        out_specs=[head_blk(blk), head_blk(blk), head_blk(v_dim)],
        compiler_params=_params("parallel", "parallel"),
        name="mla_in",
    )(x, cs, row(gain), w_in_ext, row(q_a_norm), row(kv_a_norm), wq_ext, w_kv_up.astype(BF16),
      row(q_norm[:nope]), rot_gain(q_norm[nope:]), row(k_norm[:nope]), rot_gain(k_norm[nope:]))

    tq = _tile(t, ATTN_TILE)
    full = lambda width: pl.BlockSpec((1, 1, t, width), lambda bi, hi, i: (bi, hi, 0, 0))
    o = pl.pallas_call(
        _flash_kernel,
        out_shape=jax.ShapeDtypeStruct((b, t, heads * v_dim), BF16),
        grid=(b, heads, t // tq),
        in_specs=[pl.BlockSpec((1, 1, tq, blk), lambda bi, hi, i: (bi, hi, i, 0)), full(blk), full(v_dim)],
        out_specs=pl.BlockSpec((1, tq, v_dim), lambda bi, hi, i: (bi, i, hi)),
        compiler_params=_params("parallel", "parallel", "arbitrary"),
        name="flash",
    )(q, k, v)

    m = b * t
    tmo = _tile(m, TOKEN_TILE)
    out = pl.pallas_call(
        _mla_out_kernel,
        out_shape=jax.ShapeDtypeStruct((m, d), F32),
        grid=(m // tmo,),
        in_specs=[pl.BlockSpec((tmo, d), lambda i: (i, 0)), pl.BlockSpec((tmo, heads * v_dim), lambda i: (i, 0)),
                  _resident(w_out.shape)],
        out_specs=pl.BlockSpec((tmo, d), lambda i: (i, 0)),
        compiler_params=_params("parallel"),
        name="mla_out",
    )(x.reshape(m, d), o.reshape(m, heads * v_dim), w_out.astype(BF16))
    return out.reshape(b, t, d)


def kernel(x, positions, norm_gains, ffn_w_gate, ffn_w_up, ffn_w_down, even_w_in, even_conv_w, even_mu_shift, rwkv_w0, rwkv_w_up, rwkv_a0, rwkv_a_up, rwkv_g_up, rwkv_k_k, rwkv_k_a, rwkv_r_k, rwkv_ln_w, rwkv_ln_b, even_w_out, odd_w_in, mla_q_a_norm, mla_kv_a_norm, mla_w_q_up, mla_w_kv_up, mla_q_norm, mla_k_norm, odd_w_out):
    b, t, d = x.shape
    depth = norm_gains.shape[0]
    rope_dim = odd_w_in.shape[2] - mla_q_a_norm.shape[1] - mla_kv_a_norm.shape[1]
    cs = _rope_tables(positions, rope_dim)

    def ffn(x, layer, which, gain):
        y = _ffn(x.reshape(b * t, d), gain, ffn_w_gate[layer, which].astype(BF16),
                 ffn_w_up[layer, which].astype(BF16), ffn_w_down[layer, which].astype(BF16))
        return y.reshape(b, t, d)

    for layer in range(depth):
        g = norm_gains[layer]
        x = ffn(x, layer, 0, g[0])
        if layer % 2 == 0:
            i = layer // 2
            x = _conv_rwkv_mixer(x, g[1], even_w_in[i], even_conv_w[i], even_mu_shift[i], rwkv_w0[i],
                                 rwkv_w_up[i], rwkv_a0[i], rwkv_a_up[i], rwkv_g_up[i], rwkv_k_k[i],
                                 rwkv_k_a[i], rwkv_r_k[i], rwkv_ln_w[i], rwkv_ln_b[i], even_w_out[i])
        else:
            j = layer // 2
            x = _mla_mixer(x, cs, g[1], odd_w_in[j], mla_q_a_norm[j], mla_kv_a_norm[j], mla_w_q_up[j],
                           mla_w_kv_up[j], mla_q_norm[j], mla_k_norm[j], odd_w_out[j])
        x = ffn(x, layer, 1, g[2])
    return x
```

```python
import functools
import math

import numpy as np
import jax
import jax.numpy as jnp
from jax import lax
from jax.experimental import pallas as pl
from jax.experimental.pallas import tpu as pltpu

F32 = jnp.float32
BF16 = jnp.bfloat16

NORM_EPS = 1e-6
RWKV_GN_EPS = 64e-5
ROPE_THETA = 10000.0
KK_NORM_FLOOR = 1e-24

LANES = 128
HALO = 16
VMEM_LIMIT = 56 * 1024 * 1024
RWKV_CHUNK = 64
RWKV_BLOCK = 128
TOKEN_TILE = 512
ATTN_TILE = 512

_HIGHEST = lax.Precision.HIGHEST


def _dot(a, b):
    return jnp.dot(a, b, preferred_element_type=F32)


def _dot_bf(a, b):
    return jnp.dot(a.astype(BF16), b.astype(BF16), preferred_element_type=F32)


def _dot_hp(a, b):
    return jnp.dot(a, b, preferred_element_type=F32, precision=_HIGHEST)


def _split(x, parts):
    pieces = []
    for _ in range(parts):
        piece = x.astype(BF16)
        pieces.append(piece)
        x = x - piece.astype(F32)
    return pieces


def _dot_split(x, w, parts):
    return sum(_dot(piece, w) for piece in _split(x, parts))


def _dot_split_rhs(w, x, parts):
    return sum(_dot(w, piece) for piece in _split(x, parts))


def _dot_hi_lo(x, w_ref):
    hi, lo = _split(x, 2)
    return _dot(hi, w_ref[0]) + (_dot(hi, w_ref[1]) + _dot(lo, w_ref[0]))


def _dot_nt(a, b):
    return lax.dot_general(a, b, (((1,), (1,)), ((), ())), preferred_element_type=F32)


def _dot_tn(a, b):
    return lax.dot_general(a, b, (((0,), (0,)), ((), ())), preferred_element_type=F32)


def _rms(x, gain, eps=NORM_EPS):
    return x * lax.rsqrt(jnp.mean(x * x, axis=-1, keepdims=True) + eps) * gain


def _resident(shape):
    nd = len(shape)
    return pl.BlockSpec(shape, lambda *_: (0,) * nd, pipeline_mode=pl.Buffered(1))


def _params(*sem):
    return pltpu.CompilerParams(dimension_semantics=sem, vmem_limit_bytes=VMEM_LIMIT)


def _tile(n, want):
    t = min(n, want)
    assert n % t == 0, (n, t)
    return t


def _ffn_kernel(x_ref, gain_ref, wg_ref, wu_ref, wd_ref, o_ref, act_ref, *, tf):
    x = x_ref[...]
    h = _rms(x, gain_ref[...]).astype(BF16)
    for f in range(wg_ref.shape[1] // tf):
        cols = slice(f * tf, (f + 1) * tf)
        g = _dot(h, wg_ref[:, cols])
        u = _dot(h, wu_ref[:, cols])
        act_ref[:, cols] = (g * jax.nn.sigmoid(g) * u).astype(BF16)
    o_ref[...] = x + 0.5 * _dot(act_ref[...], wd_ref[...])


def _ffn(x2d, gain, wg, wu, wd):
    m, d = x2d.shape
    f = wg.shape[1]
    tm = _tile(m, TOKEN_TILE)
    tf = 256 if f % 256 == 0 else LANES
    return pl.pallas_call(
        functools.partial(_ffn_kernel, tf=tf),
        out_shape=jax.ShapeDtypeStruct((m, d), F32),
        grid=(m // tm,),
        in_specs=[pl.BlockSpec((tm, d), lambda i: (i, 0)),
                  _resident((1, d)), _resident((d, f)), _resident((d, f)), _resident((f, d))],
        out_specs=pl.BlockSpec((tm, d), lambda i: (i, 0)),
        scratch_shapes=[pltpu.VMEM((tm, f), BF16)],
        compiler_params=_params("parallel"),
        name="ffn",
    )(x2d, gain.reshape(1, d), wg, wu, wd)


def _even_in_kernel(x_ref, xh_ref, gain_ref, w_ref, mu_ref, convw_ref, w0_ref, wup_ref, a0_ref, aup_ref,
                    gup_ref, kk_ref, ka_ref, rk_ref, hsum_ref, tri_ref, sel_ref, selt_ref,
                    a_o, r_o, b_o, k_o, v_o, bw_o, kw_o, wend_o, g_o, yc_o, bonus_o,
                    h_s, p_s, *, cw, rw):
    tm = x_ref.shape[1]
    gain = gain_ref[...]
    xh = jnp.where(pl.program_id(1) > 0, xh_ref[0], 0.0)
    h_s[0:HALO, :] = _rms(xh, gain).astype(BF16)
    h_s[HALO:, :] = _rms(x_ref[0], gain).astype(BF16)

    def proj(c0, n):
        return _dot(h_s[...], w_ref[:, c0:c0 + n])

    def staged(back, n):
        return p_s[pl.ds(HALO - back, tm), 0:n]

    gate_b = proj(0, cw)[HALO:]
    p_s[:, 0:cw] = proj(cw, cw) * proj(2 * cw, cw)
    cwt = convw_ref[...]
    yc_o[0] = gate_b * (staged(2, cw) * cwt[0:1] + staged(1, cw) * cwt[1:2] + staged(0, cw) * cwt[2:3])

    def shifted(c0, n):
        p_s[:, 0:n] = proj(c0, n)
        cur = staged(0, n)
        return cur + (staged(1, n) - cur) * mu_ref[:, c0 - 3 * cw:c0 - 3 * cw + n]

    base = 3 * cw
    r = shifted(base, rw)
    k = shifted(base + rw, rw)
    v = shifted(base + 2 * rw, rw)
    lora = shifted(base + 3 * rw, 3 * LANES)
    dw, da, dg = lora[:, 0:LANES], lora[:, LANES:2 * LANES], lora[:, 2 * LANES:3 * LANES]

    zw = w0_ref[...] + _dot_hi_lo(jnp.tanh(dw), wup_ref)
    lw = -math.exp(-0.5) * jax.nn.sigmoid(zw)
    iclr = jax.nn.sigmoid(a0_ref[...] + _dot_hi_lo(da, aup_ref))
    g_o[0] = _dot_hi_lo(jax.nn.sigmoid(dg), gup_ref)
    hsum = hsum_ref[...]
    kkr = k * kk_ref[...]
    kk = kkr * lax.rsqrt(jnp.maximum(_dot_split(kkr * kkr, hsum, 2), KK_NORM_FLOOR))
    kh = k * (1.0 + (iclr - 1.0) * ka_ref[...])
    bonus_o[0] = _dot_split(r * kh * rk_ref[...], hsum, 2) * v

    cum = _dot_split_rhs(tri_ref[...], lw, 3)
    tot = _dot_split_rhs(sel_ref[...], lw, 3)
    rest = _dot_split_rhs(selt_ref[...], tot, 3) - cum
    grow = jnp.exp(-cum)
    to_end = jnp.exp(rest)
    b_vec = kk * iclr
    a_o[0] = (-kk * jnp.exp(cum - lw)).astype(BF16)
    r_o[0] = (r * jnp.exp(cum)).astype(BF16)
    b_o[0] = (b_vec * grow).astype(BF16)
    k_o[0] = (kh * grow).astype(BF16)
    v_o[0] = v.astype(BF16)
    bw_o[0] = (b_vec * to_end).astype(BF16)
    kw_o[0] = (kh * to_end).astype(BF16)
    chunk_decay = jnp.exp(tot)
    for c in range(tot.shape[0]):
        wend_o[0, c] = chunk_decay[c:c + 1, :]


def _rwkv_chunk_kernel(a_ref, r_ref, b_ref, k_ref, v_ref, bw_ref, kw_ref, wend_ref, y_ref, st_ref, *, chunk):
    @pl.when(pl.program_id(1) == 0)
    def _():
        st_ref[...] = jnp.zeros_like(st_ref)

    tc, width = a_ref.shape[1], a_ref.shape[2]
    c2 = 2 * chunk
    assert c2 == LANES
    n_pairs = width // LANES
    units = [(c, p) for c in range(tc // chunk) for p in range(n_pairs)]
    row = lax.broadcasted_iota(jnp.int32, (c2, c2), 0)
    col = lax.broadcasted_iota(jnp.int32, (c2, c2), 1)
    same = (row // chunk) == (col // chunk)
    strict = same & (col < row)
    incl = same & (col <= row)
    eye = row == col
    head0 = lax.broadcasted_iota(jnp.int32, (chunk, LANES), 1) < chunk

    def stacked(ref):
        out = []
        for c, p in units:
            x = ref[0, c * chunk:(c + 1) * chunk, p * LANES:(p + 1) * LANES]
            zero = jnp.zeros_like(x)
            out.append(jnp.concatenate([jnp.where(head0, x, zero), jnp.where(head0, zero, x)], axis=0))
        return out

    a_s, r_s, b_s, k_s, v_s, bw_s, kw_s = (stacked(ref) for ref in (a_ref, r_ref, b_ref, k_ref, v_ref, bw_ref, kw_ref))
    each = lambda fn: [fn(u) for u in range(len(units))]
    att = each(lambda u: _dot_nt(jnp.concatenate([a_s[u], r_s[u]], axis=0), jnp.concatenate([b_s[u], k_s[u]], axis=0)))
    power = each(lambda u: jnp.where(strict, att[u][:c2, :c2], 0.0).astype(BF16))
    a_ak = each(lambda u: jnp.where(strict, att[u][:c2, c2:], 0.0).astype(BF16))
    a_rb = each(lambda u: jnp.where(incl, att[u][c2:, :c2], 0.0).astype(BF16))
    a_rk = each(lambda u: jnp.where(incl, att[u][c2:, c2:], 0.0).astype(BF16))
    x = each(lambda u: jnp.concatenate([a_s[u].astype(F32), _dot(a_ak[u], v_s[u])], axis=1))
    factors = int(math.log2(chunk))
    for f in range(factors):
        x = each(lambda u: x[u] + _dot(power[u], x[u].astype(BF16)))
        if f + 1 < factors:
            power = each(lambda u: _dot(power[u], power[u]).astype(BF16))
    pq = each(lambda u: x[u].astype(BF16))
    gy = each(lambda u: _dot(a_rb[u], pq[u]))
    mn = each(lambda u: _dot_tn(bw_s[u], pq[u]))
    g_mat = each(lambda u: (r_s[u].astype(F32) + gy[u][:, :c2]).astype(BF16))
    y0 = each(lambda u: gy[u][:, c2:] + _dot(a_rk[u], v_s[u]))
    n0 = each(lambda u: mn[u][:, c2:] + _dot_tn(kw_s[u], v_s[u]))

    def decay_plus(u):
        c, p = units[u]
        w_end = wend_ref[0, c, :, p * LANES:(p + 1) * LANES]
        return (jnp.where(eye, w_end, 0.0) + mn[u][:, :c2]).astype(BF16)

    m_mat = each(decay_plus)
    state = [st_ref[p] for p in range(n_pairs)]
    for u, (c, p) in enumerate(units):
        s_bf = state[p].astype(BF16)
        ys = _dot(g_mat[u], s_bf) + y0[u]
        state[p] = _dot(m_mat[u], s_bf) + n0[u]
        y_ref[0, c * chunk:(c + 1) * chunk, p * LANES:(p + 1) * LANES] = ys[:chunk] + ys[chunk:]
    for p in range(n_pairs):
        st_ref[p] = state[p]


def _even_out_kernel(x_ref, y_ref, bonus_ref, g_ref, yc_ref, lnw_ref, lnb_ref, hmean_ref, wo_ref, o_ref, *, cw):
    y = y_ref[...]
    hmean = hmean_ref[...]
    d = y - _dot_split(y, hmean, 2)
    var = _dot_split(d * d, hmean, 2)
    yn = d * lax.rsqrt(var + RWKV_GN_EPS) * lnw_ref[...] + lnb_ref[...]
    y_rwkv = (yn + bonus_ref[...]) * g_ref[...]
    mix = _dot(yc_ref[...].astype(BF16), wo_ref[0:cw, :]) + _dot(y_rwkv.astype(BF16), wo_ref[cw:, :])
    o_ref[...] = x_ref[...] + mix


def _conv_rwkv_mixer(x, gain, w_in, conv_w, mu_shift, w0, w_up, a0, a_up, g_up, k_k, k_a, r_k, ln_w, ln_b, w_out):
    b, t, d = x.shape
    cw = conv_w.shape[1]
    heads, hd = r_k.shape
    rw = heads * hd
    ranks = (w_up.shape[0], a_up.shape[0], g_up.shape[0])
    assert max(ranks) <= LANES and rw % LANES == 0 and cw == rw and LANES % hd == 0

    base = 3 * cw + 3 * rw
    offs = (base, base + ranks[0], base + ranks[0] + ranks[1])

    def padded_cols(arr):
        blocks = [jnp.pad(arr[..., o:o + n], [(0, 0)] * (arr.ndim - 1) + [(0, LANES - n)])
                  for o, n in zip(offs, ranks)]
        return jnp.concatenate([arr[..., :base]] + blocks, axis=-1)

    w_cat = padded_cols(w_in).astype(BF16)
    mu_cat = padded_cols(jnp.pad(mu_shift.reshape(1, -1), ((0, 0), (3 * cw, 0))))[:, 3 * cw:]
    def pad_rows(m):
        m = jnp.pad(m, ((0, LANES - m.shape[0]), (0, 0)))
        hi = m.astype(BF16)
        return jnp.stack([hi, (m - hi.astype(F32)).astype(BF16)])

    head_id = np.arange(rw) // hd
    hsum = jnp.asarray((head_id[:, None] == head_id[None, :]).astype(np.float32)).astype(BF16)
    row = lambda vec: vec.reshape(1, -1)

    tm = _tile(t, TOKEN_TILE)
    nt = t // tm
    wide = w_cat.shape[1]
    assert tm % RWKV_CHUNK == 0
    chunks = tm // RWKV_CHUNK
    chunk_id = np.arange(tm) // RWKV_CHUNK
    tri = jnp.asarray(((chunk_id[:, None] == chunk_id[None, :])
                       & (np.arange(tm)[None, :] <= np.arange(tm)[:, None])).astype(np.float32)).astype(BF16)
    sel = jnp.asarray((np.arange(chunks)[:, None] == chunk_id[None, :]).astype(np.float32)).astype(BF16)
    tok = pl.BlockSpec((1, tm, rw), lambda bi, i: (bi, i, 0))
    seq = jax.ShapeDtypeStruct((b, t, rw), F32)
    seq_bf = jax.ShapeDtypeStruct((b, t, rw), BF16)
    a_sc, r_sc, b_sc, k_sc, v_bf, b_end, k_end, w_end, gate, y_conv, bonus = pl.pallas_call(
        functools.partial(_even_in_kernel, cw=cw, rw=rw),
        out_shape=[seq_bf] * 7 + [jax.ShapeDtypeStruct((b, t // RWKV_CHUNK, 1, rw), F32)] + [seq] * 3,
        grid=(b, nt),
        in_specs=[pl.BlockSpec((1, tm, d), lambda bi, i: (bi, i, 0)),
                  pl.BlockSpec((1, HALO, d), lambda bi, i: (bi, jnp.maximum(i * (tm // HALO) - 1, 0), 0)),
                  _resident((1, d)), _resident((d, wide)), _resident((1, wide - 3 * cw)),
                  _resident(conv_w.shape), _resident((1, rw)), _resident((2, LANES, rw)),
                  _resident((1, rw)), _resident((2, LANES, rw)), _resident((2, LANES, rw)),
                  _resident((1, rw)), _resident((1, rw)), _resident((1, rw)), _resident((rw, rw)),
                  _resident((tm, tm)), _resident((chunks, tm)), _resident((tm, chunks))],
        out_specs=[tok] * 7 + [pl.BlockSpec((1, chunks, 1, rw), lambda bi, i: (bi, i, 0, 0))] + [tok] * 3,
        scratch_shapes=[pltpu.VMEM((tm + HALO, d), BF16), pltpu.VMEM((tm + HALO, max(cw, rw)), F32)],
        compiler_params=_params("parallel", "parallel"),
        name="even_in",
    )(x, x, row(gain), w_cat, mu_cat, conv_w, row(w0), pad_rows(w_up), row(a0), pad_rows(a_up),
      pad_rows(g_up), row(k_k), row(k_a), row(r_k), hsum, tri, sel, sel.T)

    tc = _tile(t, RWKV_BLOCK)
    blk = pl.BlockSpec((1, tc, rw), lambda bi, j: (bi, j, 0))
    y = pl.pallas_call(
        functools.partial(_rwkv_chunk_kernel, chunk=RWKV_CHUNK),
        out_shape=seq,
        grid=(b, t // tc),
        in_specs=[blk] * 7 + [pl.BlockSpec((1, tc // RWKV_CHUNK, 1, rw), lambda bi, j: (bi, j, 0, 0))],
        out_specs=blk,
        scratch_shapes=[pltpu.VMEM((rw // LANES, LANES, LANES), F32)],
        compiler_params=_params("parallel", "arbitrary"),
        name="rwkv_chunk",
    )(a_sc, r_sc, b_sc, k_sc, v_bf, b_end, k_end, w_end)

    m = b * t
    tmo = _tile(m, TOKEN_TILE)
    flat = lambda z: z.reshape(m, -1)
    tokf = pl.BlockSpec((tmo, rw), lambda i: (i, 0))
    out = pl.pallas_call(
        functools.partial(_even_out_kernel, cw=cw),
        out_shape=jax.ShapeDtypeStruct((m, d), F32),
        grid=(m // tmo,),
        in_specs=[pl.BlockSpec((tmo, d), lambda i: (i, 0)), tokf, tokf, tokf, tokf,
                  _resident((1, rw)), _resident((1, rw)), _resident((rw, rw)), _resident((cw + rw, d))],
        out_specs=pl.BlockSpec((tmo, d), lambda i: (i, 0)),
        compiler_params=_params("parallel"),
        name="even_out",
    )(flat(x), flat(y), flat(bonus), flat(gate), flat(y_conv), row(ln_w), row(ln_b), hsum / hd,
      w_out.astype(BF16))
    return out.reshape(b, t, d)


def _rope_table_kernel(pos_ref, freq_ref, cos_ref, sin_ref):
    ang = pos_ref[...].astype(F32) * freq_ref[...]
    cos_ref[...] = jnp.cos(ang)
    sin_ref[...] = jnp.sin(ang)


def _rope_tables(positions, rope_dim):
    b, t = positions.shape
    half = rope_dim // 2
    assert LANES % half == 0
    per_row = LANES // half
    rows = b * t // per_row
    inv_freq = ROPE_THETA ** (-jnp.arange(0, rope_dim, 2, dtype=F32) / rope_dim)
    pos_rep = jnp.repeat(positions.reshape(-1), half).reshape(rows, LANES)
    tr = _tile(rows, 1024)
    blk = pl.BlockSpec((tr, LANES), lambda i: (i, 0))
    cos, sin = pl.pallas_call(
        _rope_table_kernel,
        out_shape=[jax.ShapeDtypeStruct((rows, LANES), F32)] * 2,
        grid=(rows // tr,),
        in_specs=[blk, _resident((1, LANES))],
        out_specs=[blk, blk],
        compiler_params=_params("parallel"),
        name="rope_table",
    )(pos_rep, jnp.tile(inv_freq, per_row).reshape(1, LANES))
    cos = cos.reshape(b, t, half)
    sin = sin.reshape(b, t, half)
    return jnp.concatenate([cos] * per_row + [sin] * per_row, axis=-1)


def _mla_in_kernel(x_ref, cs_ref, gain_ref, win_ref, qa_ref, kva_ref, wq_ref, wkv_ref,
                   gqn_ref, gqa_ref, gqb_ref, gkn_ref, gka_ref, gkb_ref, ones_ref, q_o, k_o, v_o,
                   *, heads, q_rank, kv_rank, nope, rope, scale):
    qk_dim = nope + rope
    tm = x_ref.shape[1]
    h = _rms(x_ref[0], gain_ref[...]).astype(BF16)
    p = _dot(h, win_ref[...])
    c_q = _rms(p[:, :q_rank], qa_ref[...]).astype(BF16)
    c_kv = _rms(p[:, q_rank:q_rank + kv_rank], kva_ref[...]).astype(BF16)
    base = q_rank + kv_rank
    k_a, k_b = p[:, base:base + LANES], p[:, base + LANES:base + 2 * LANES]
    q = _dot(c_q, wq_ref[...])
    kv = _dot(c_kv, wkv_ref[...])
    cos, sin = cs_ref[0, :, 0:LANES], cs_ref[0, :, LANES:2 * LANES]
    lane = lax.broadcasted_iota(jnp.int32, (tm, LANES), 1)
    first = lane < rope
    ones = ones_ref[...]
    one_hot = jnp.where(lane == 0, 1.0, 0.0).astype(BF16)
    pairs = heads // 2
    rope_a = [q[:, (heads + j) * LANES:(heads + j + 1) * LANES] for j in range(pairs)]
    rope_b = [q[:, (heads + pairs + j) * LANES:(heads + pairs + j + 1) * LANES] for j in range(pairs)]
    q_rot = [rope_a[j] * gqa_ref[...] * cos + rope_b[j] * gqb_ref[...] * sin for j in range(pairs)]
    k_rot = k_a * gka_ref[...] * cos + k_b * gkb_ref[...] * sin
    k_rot_sq = jnp.where(first, k_a * k_a, 0.0)
    mine = [first if hd % 2 == 0 else jnp.logical_not(first) for hd in range(heads)]
    q_n = [q[:, hd * LANES:(hd + 1) * LANES] for hd in range(heads)]
    k_n = [kv[:, hd * 2 * LANES:hd * 2 * LANES + nope] for hd in range(heads)]
    ss_q = [_dot(jnp.concatenate([q_n[hd] * q_n[hd], jnp.where(mine[hd], rope_a[hd // 2] * rope_a[hd // 2], 0.0)],
                                 axis=1).astype(BF16), ones) for hd in range(heads)]
    ss_k = [_dot(jnp.concatenate([k_n[hd] * k_n[hd], k_rot_sq], axis=1).astype(BF16), ones) for hd in range(heads)]
    for hd in range(heads):
        rs_q = lax.rsqrt(ss_q[hd] / qk_dim + NORM_EPS) * scale
        rs_k = lax.rsqrt(ss_k[hd] / qk_dim + NORM_EPS)
        q_o[0, hd, :, 0:nope] = (q_n[hd] * rs_q * gqn_ref[...]).astype(BF16)
        q_o[0, hd, :, nope:nope + LANES] = (jnp.where(mine[hd], q_rot[hd // 2], 0.0) * rs_q).astype(BF16)
        k_o[0, hd, :, 0:nope] = (k_n[hd] * rs_k * gkn_ref[...]).astype(BF16)
        k_o[0, hd, :, nope:nope + LANES] = (k_rot * rs_k).astype(BF16)
        v_o[0, hd, :, 0:LANES] = kv[:, hd * 2 * LANES + nope:(hd + 1) * 2 * LANES].astype(BF16)
        v_o[0, hd, :, LANES:2 * LANES] = one_hot


def _flash_kernel(q_ref, k_ref, v_ref, o_ref, sa_ref, sb_ref, m_ref, acc_ref):
    i = pl.program_id(2)
    tq = q_ref.shape[2]
    vd = o_ref.shape[2]

    def block(j):
        return pl.ds(pl.multiple_of(j * tq, tq), tq)

    def logits(j):
        return _dot_nt(q_ref[0, 0], k_ref[0, 0, block(j), :])

    def consume(j, s):
        m = m_ref[...]
        m_new = jnp.maximum(m, jnp.max(s, axis=-1, keepdims=True))
        pr = jnp.exp2(s - m_new).astype(BF16)
        acc_ref[...] = jnp.exp2(m - m_new) * acc_ref[...] + _dot(pr, v_ref[0, 0, block(j), :])
        m_ref[...] = m_new

    def causal(s):
        keep = lax.broadcasted_iota(jnp.int32, s.shape, 1) <= lax.broadcasted_iota(jnp.int32, s.shape, 0)
        return jnp.where(keep, s, -jnp.inf)

    m_ref[...] = jnp.full(m_ref.shape, -jnp.inf, F32)
    acc_ref[...] = jnp.zeros(acc_ref.shape, F32)
    sa_ref[...] = logits(0)

    def pair(t, carry):
        j = 2 * t
        sb_ref[...] = logits(j + 1)
        consume(j, sa_ref[...])
        sa_ref[...] = logits(j + 2)
        consume(j + 1, sb_ref[...])
        return carry

    lax.fori_loop(0, i // 2, pair, 0)

    @pl.when(i % 2 == 0)
    def _():
        consume(i, causal(sa_ref[...]))

    @pl.when(i % 2 == 1)
    def _():
        sb_ref[...] = logits(i)
        consume(i - 1, sa_ref[...])
        consume(i, causal(sb_ref[...]))

    acc = acc_ref[...]
    o_ref[0] = (acc[:, :vd] / acc[:, vd:vd + 1]).astype(o_ref.dtype)


def _mla_out_kernel(x_ref, o_ref, wo_ref, out_ref):
    out_ref[...] = x_ref[...] + _dot(o_ref[...], wo_ref[...])


def _mla_mixer(x, cs, gain, w_in, q_a_norm, kv_a_norm, w_q_up, w_kv_up, q_norm, k_norm, w_out):
    b, t, d = x.shape
    q_rank, kv_rank, qk_dim = q_a_norm.shape[0], kv_a_norm.shape[0], q_norm.shape[0]
    rope = w_in.shape[1] - q_rank - kv_rank
    nope = qk_dim - rope
    heads = w_q_up.shape[1] // qk_dim
    v_dim = w_kv_up.shape[1] // heads - nope
    half = rope // 2
    blk = nope + LANES
    assert nope == LANES and v_dim == LANES and 2 * rope == LANES and heads % 2 == 0
    assert q_rank % LANES == 0 and kv_rank % LANES == 0

    def rot_cols(w):
        return jnp.concatenate([-w[..., half:], w[..., :half]], axis=-1)

    def rot_gain(g):
        return jnp.concatenate([g[half:], g[:half]])

    twice = lambda z: jnp.concatenate([z, z], axis=-1)
    w_pe = w_in[:, q_rank + kv_rank:]
    w_in_ext = jnp.concatenate([w_in[:, :q_rank + kv_rank], twice(w_pe), twice(rot_cols(w_pe))], axis=1).astype(BF16)
    wq = w_q_up.reshape(q_rank, heads, qk_dim)
    wq_ext = jnp.concatenate([wq[..., :nope].reshape(q_rank, heads * nope),
                              wq[..., nope:].reshape(q_rank, heads * rope),
                              rot_cols(wq[..., nope:]).reshape(q_rank, heads * rope)], axis=1).astype(BF16)
    row = lambda vec: vec.reshape(1, -1)
    ones = jnp.ones((blk, LANES), BF16)

    tm = _tile(t, TOKEN_TILE)
    head_blk = pl.BlockSpec((1, heads, tm, blk), lambda bi, i: (bi, 0, i, 0))
    head_arr = jax.ShapeDtypeStruct((b, heads, t, blk), BF16)
    lane_row = _resident((1, LANES))
    q, k, v = pl.pallas_call(
        functools.partial(_mla_in_kernel, heads=heads, q_rank=q_rank, kv_rank=kv_rank, nope=nope, rope=rope,
                          scale=qk_dim ** -0.5 * math.log2(math.e)),
        out_shape=[head_arr] * 3,
        grid=(b, t // tm),
        in_specs=[pl.BlockSpec((1, tm, d), lambda bi, i: (bi, i, 0)),
                  pl.BlockSpec((1, tm, 2 * LANES), lambda bi, i: (bi, i, 0)),
                  _resident((1, d)), _resident(w_in_ext.shape), _resident((1, q_rank)), _resident((1, kv_rank)),
                  _resident(wq_ext.shape), _resident(w_kv_up.shape),
                  lane_row, lane_row, lane_row, lane_row, lane_row, lane_row, _resident(ones.shape)],
        out_specs=[head_blk] * 3,
        compiler_params=_params("parallel", "parallel"),
        name="mla_in",
    )(x, cs, row(gain), w_in_ext, row(q_a_norm), row(kv_a_norm), wq_ext, w_kv_up.astype(BF16),
      row(q_norm[:nope]), row(twice(q_norm[nope:])), row(twice(rot_gain(q_norm[nope:]))),
      row(k_norm[:nope]), row(twice(k_norm[nope:])), row(twice(rot_gain(k_norm[nope:]))), ones)

    tq = _tile(t, ATTN_TILE)
    full = lambda width: pl.BlockSpec((1, 1, t, width), lambda bi, hi, i: (bi, hi, 0, 0))
    o = pl.pallas_call(
        _flash_kernel,
        out_shape=jax.ShapeDtypeStruct((b, t, heads * v_dim), BF16),
        grid=(b, heads, t // tq),
        in_specs=[pl.BlockSpec((1, 1, tq, blk), lambda bi, hi, i: (bi, hi, i, 0)), full(blk), full(blk)],
        scratch_shapes=[pltpu.VMEM((tq, tq), F32), pltpu.VMEM((tq, tq), F32), pltpu.VMEM((tq, 1), F32),
                        pltpu.VMEM((tq, blk), F32)],
        out_specs=pl.BlockSpec((1, tq, v_dim), lambda bi, hi, i: (bi, i, hi)),
        compiler_params=_params("parallel", "parallel", "arbitrary"),
        name="flash",
    )(q, k, v)

    m = b * t
    tmo = _tile(m, TOKEN_TILE)
    out = pl.pallas_call(
        _mla_out_kernel,
        out_shape=jax.ShapeDtypeStruct((m, d), F32),
        grid=(m // tmo,),
        in_specs=[pl.BlockSpec((tmo, d), lambda i: (i, 0)), pl.BlockSpec((tmo, heads * v_dim), lambda i: (i, 0)),
                  _resident(w_out.shape)],
        out_specs=pl.BlockSpec((tmo, d), lambda i: (i, 0)),
        compiler_params=_params("parallel"),
        name="mla_out",
    )(x.reshape(m, d), o.reshape(m, heads * v_dim), w_out.astype(BF16))
    return out.reshape(b, t, d)


def kernel(x, positions, norm_gains, ffn_w_gate, ffn_w_up, ffn_w_down, even_w_in, even_conv_w, even_mu_shift, rwkv_w0, rwkv_w_up, rwkv_a0, rwkv_a_up, rwkv_g_up, rwkv_k_k, rwkv_k_a, rwkv_r_k, rwkv_ln_w, rwkv_ln_b, even_w_out, odd_w_in, mla_q_a_norm, mla_kv_a_norm, mla_w_q_up, mla_w_kv_up, mla_q_norm, mla_k_norm, odd_w_out):
    b, t, d = x.shape
    depth = norm_gains.shape[0]
    rope_dim = odd_w_in.shape[2] - mla_q_a_norm.shape[1] - mla_kv_a_norm.shape[1]
    cs = _rope_tables(positions, rope_dim)

    def ffn(x, layer, which, gain):
        y = _ffn(x.reshape(b * t, d), gain, ffn_w_gate[layer, which].astype(BF16),
                 ffn_w_up[layer, which].astype(BF16), ffn_w_down[layer, which].astype(BF16))
        return y.reshape(b, t, d)

    for layer in range(depth):
        g = norm_gains[layer]
        x = ffn(x, layer, 0, g[0])
        if layer % 2 == 0:
            i = layer // 2
            x = _conv_rwkv_mixer(x, g[1], even_w_in[i], even_conv_w[i], even_mu_shift[i], rwkv_w0[i],
                                 rwkv_w_up[i], rwkv_a0[i], rwkv_a_up[i], rwkv_g_up[i], rwkv_k_k[i],
                                 rwkv_k_a[i], rwkv_r_k[i], rwkv_ln_w[i], rwkv_ln_b[i], even_w_out[i])
        else:
            j = layer // 2
            x = _mla_mixer(x, cs, g[1], odd_w_in[j], mla_q_a_norm[j], mla_kv_a_norm[j], mla_w_q_up[j],
                           mla_w_kv_up[j], mla_q_norm[j], mla_k_norm[j], odd_w_out[j])
        x = ffn(x, layer, 1, g[2])
    return x
```

```python
import functools
import math

import numpy as np
import jax
import jax.numpy as jnp
from jax import lax
from jax.experimental import pallas as pl
from jax.experimental.pallas import tpu as pltpu

F32 = jnp.float32
BF16 = jnp.bfloat16

NORM_EPS = 1e-6
RWKV_GN_EPS = 64e-5
ROPE_THETA = 10000.0
KK_NORM_FLOOR = 1e-24

LANES = 128
HALO = 16
VMEM_LIMIT = 56 * 1024 * 1024
RWKV_CHUNK = 64
RWKV_BLOCK = 256
TOKEN_TILE = 512
ATTN_TILE = 512
ATTN_HEADS_PER_STEP = 2


def _dot(a, b):
    return jnp.dot(a, b, preferred_element_type=F32)


def _dot_bf(a, b):
    return jnp.dot(a.astype(BF16), b.astype(BF16), preferred_element_type=F32)


def _dot_nt(a, b):
    return lax.dot_general(a, b, (((1,), (1,)), ((), ())), preferred_element_type=F32)


def _dot_tn(a, b):
    return lax.dot_general(a, b, (((0,), (0,)), ((), ())), preferred_element_type=F32)


def _split(x, parts):
    pieces = []
    for _ in range(parts):
        piece = x.astype(BF16)
        pieces.append(piece)
        x = x - piece.astype(F32)
    return pieces


def _dot_split_rhs(w, x, parts):
    return sum(_dot(w, piece) for piece in _split(x, parts))


def _dot_hi_lo(x, w_ref):
    hi, lo = _split(x, 2)
    return _dot(hi, w_ref[0]) + (_dot(hi, w_ref[1]) + _dot(lo, w_ref[0]))


def _rms(x, gain, eps=NORM_EPS):
    return x * lax.rsqrt(jnp.mean(x * x, axis=-1, keepdims=True) + eps) * gain


def _resident(shape):
    nd = len(shape)
    return pl.BlockSpec(shape, lambda *_: (0,) * nd, pipeline_mode=pl.Buffered(1))


def _params(*sem):
    return pltpu.CompilerParams(dimension_semantics=sem, vmem_limit_bytes=VMEM_LIMIT)


def _tile(n, want):
    t = min(n, want)
    assert n % t == 0, (n, t)
    return t


def _ffn_kernel(*refs, tf, n_mix, mix_fn):
    x_ref, mix_refs = refs[0], refs[1:1 + n_mix]
    gain_ref, wg_ref, wu_ref, wd_ref, o_ref, act_ref = refs[1 + n_mix:]
    x = x_ref[...]
    if mix_fn is not None:
        x = x + mix_fn(*mix_refs)
    h = _rms(x, gain_ref[...]).astype(BF16)
    for f in range(wg_ref.shape[1] // tf):
        cols = slice(f * tf, (f + 1) * tf)
        g = _dot(h, wg_ref[:, cols])
        u = _dot(h, wu_ref[:, cols])
        act_ref[:, cols] = (g * jax.nn.sigmoid(g) * u).astype(BF16)
    o_ref[...] = x + 0.5 * _dot(act_ref[...], wd_ref[...])


def _ffn(x2d, gain, wg, wu, wd, mix=None):
    m, d = x2d.shape
    f = wg.shape[1]
    tm = _tile(m, TOKEN_TILE)
    tf = 256 if f % 256 == 0 else LANES
    mix_fn, mix_ops, mix_specs = mix if mix is not None else (None, [], [])
    return pl.pallas_call(
        functools.partial(_ffn_kernel, tf=tf, n_mix=len(mix_ops), mix_fn=mix_fn),
        out_shape=jax.ShapeDtypeStruct((m, d), F32),
        grid=(m // tm,),
        in_specs=[pl.BlockSpec((tm, d), lambda i: (i, 0))] + list(mix_specs)
                 + [_resident((1, d)), _resident((d, f)), _resident((d, f)), _resident((f, d))],
        out_specs=pl.BlockSpec((tm, d), lambda i: (i, 0)),
        scratch_shapes=[pltpu.VMEM((tm, f), BF16)],
        compiler_params=_params("parallel"),
        name="ffn",
    )(x2d, *mix_ops, gain.reshape(1, d), wg, wu, wd)


def _even_in_kernel(x_ref, xh_ref, gain_ref, w_ref, mu_ref, convw_ref, w0_ref, wup_ref, a0_ref, aup_ref,
                    gup_ref, kk_ref, ka_ref, rk_ref, hsum_ref, tri_ref,
                    a_o, r_o, b_o, k_o, v_o, bw_o, kw_o, wend_o, g_o, yc_o, bonus_o,
                    h_s, p_s, *, cw, rw, chunk):
    tm = x_ref.shape[1]
    gain = gain_ref[...]
    xh = jnp.where(pl.program_id(1) > 0, xh_ref[0], 0.0)
    h_s[0:HALO, :] = _rms(xh, gain).astype(BF16)
    h_s[HALO:, :] = _rms(x_ref[0], gain).astype(BF16)

    def proj(c0, n):
        return _dot(h_s[...], w_ref[:, c0:c0 + n])

    def staged(back, n):
        return p_s[pl.ds(HALO - back, tm), 0:n]

    gate_b = proj(0, cw)[HALO:]
    p_s[:, 0:cw] = proj(cw, cw) * proj(2 * cw, cw)
    cwt = convw_ref[...]
    yc_o[0] = (gate_b * (staged(2, cw) * cwt[0:1] + staged(1, cw) * cwt[1:2] + staged(0, cw) * cwt[2:3])).astype(BF16)

    def shifted(c0, n):
        p_s[:, 0:n] = proj(c0, n)
        cur = staged(0, n)
        return cur + (staged(1, n) - cur) * mu_ref[:, c0 - 3 * cw:c0 - 3 * cw + n]

    base = 3 * cw
    r = shifted(base, rw)
    k = shifted(base + rw, rw)
    v = shifted(base + 2 * rw, rw)
    lora = shifted(base + 3 * rw, 3 * LANES)
    dw, da, dg = lora[:, 0:LANES], lora[:, LANES:2 * LANES], lora[:, 2 * LANES:3 * LANES]

    zw = w0_ref[...] + _dot_hi_lo(jnp.tanh(dw), wup_ref)
    lw = -math.exp(-0.5) * jax.nn.sigmoid(zw)
    iclr = jax.nn.sigmoid(a0_ref[...] + _dot_bf(da, aup_ref[0]))
    g_o[0] = _dot_bf(jax.nn.sigmoid(dg), gup_ref[0]).astype(BF16)
    hsum = hsum_ref[...]
    kkr = k * kk_ref[...]
    kk = kkr * lax.rsqrt(jnp.maximum(_dot_bf(kkr * kkr, hsum), KK_NORM_FLOOR))
    kh = k * (1.0 + (iclr - 1.0) * ka_ref[...])
    bonus_o[0] = (_dot_bf(r * kh * rk_ref[...], hsum) * v).astype(BF16)
    b_vec = kk * iclr
    v_o[0] = v.astype(BF16)

    tri = tri_ref[...]
    for c in range(tm // chunk):
        rows = slice(c * chunk, (c + 1) * chunk)
        lw_c = lw[rows]
        cum = _dot_split_rhs(tri, lw_c, 2)
        tot = cum[chunk - 1:chunk, :]
        grow = jnp.exp(-cum)
        to_end = jnp.exp(tot - cum)
        a_o[0, rows, :] = (-kk[rows] * jnp.exp(cum - lw_c)).astype(BF16)
        r_o[0, rows, :] = (r[rows] * jnp.exp(cum)).astype(BF16)
        b_o[0, rows, :] = (b_vec[rows] * grow).astype(BF16)
        k_o[0, rows, :] = (kh[rows] * grow).astype(BF16)
        bw_o[0, rows, :] = (b_vec[rows] * to_end).astype(BF16)
        kw_o[0, rows, :] = (kh[rows] * to_end).astype(BF16)
        wend_o[0, c] = jnp.exp(tot)


def _rwkv_chunk_kernel(a_ref, r_ref, b_ref, k_ref, v_ref, bw_ref, kw_ref, wend_ref, y_ref, st_ref, *, chunk):
    @pl.when(pl.program_id(1) == 0)
    def _():
        st_ref[...] = jnp.zeros_like(st_ref)

    tc, width = a_ref.shape[1], a_ref.shape[2]
    c2 = 2 * chunk
    assert c2 == LANES
    n_pairs = width // LANES
    units = [(c, p) for c in range(tc // chunk) for p in range(n_pairs)]
    row = lax.broadcasted_iota(jnp.int32, (c2, c2), 0)
    col = lax.broadcasted_iota(jnp.int32, (c2, c2), 1)
    same = (row // chunk) == (col // chunk)
    strict = same & (col < row)
    incl = same & (col <= row)
    eye = row == col
    head0 = lax.broadcasted_iota(jnp.int32, (chunk, LANES), 1) < chunk

    def stacked(ref):
        out = []
        for c, p in units:
            x = ref[0, c * chunk:(c + 1) * chunk, p * LANES:(p + 1) * LANES]
            zero = jnp.zeros_like(x)
            out.append(jnp.concatenate([jnp.where(head0, x, zero), jnp.where(head0, zero, x)], axis=0))
        return out

    a_s, r_s, b_s, k_s, v_s, bw_s, kw_s = (stacked(ref) for ref in (a_ref, r_ref, b_ref, k_ref, v_ref, bw_ref, kw_ref))
    each = lambda fn: [fn(u) for u in range(len(units))]
    att = each(lambda u: _dot_nt(jnp.concatenate([a_s[u], r_s[u]], axis=0), jnp.concatenate([b_s[u], k_s[u]], axis=0)))
    power = each(lambda u: jnp.where(strict, att[u][:c2, :c2], 0.0).astype(BF16))
    a_ak = each(lambda u: jnp.where(strict, att[u][:c2, c2:], 0.0).astype(BF16))
    a_rb = each(lambda u: jnp.where(incl, att[u][c2:, :c2], 0.0).astype(BF16))
    a_rk = each(lambda u: jnp.where(incl, att[u][c2:, c2:], 0.0).astype(BF16))
    x = each(lambda u: jnp.concatenate([a_s[u].astype(F32), _dot(a_ak[u], v_s[u])], axis=1))
    factors = int(math.log2(chunk))
    for f in range(factors):
        x = each(lambda u: x[u] + _dot(power[u], x[u].astype(BF16)))
        if f + 1 < factors:
            power = each(lambda u: _dot(power[u], power[u]).astype(BF16))
    zero = jnp.zeros((c2, c2), BF16)
    pqv = each(lambda u: jnp.concatenate([x[u].astype(BF16), jnp.concatenate([zero, v_s[u]], axis=1)], axis=0))
    gy = each(lambda u: _dot(jnp.concatenate([a_rb[u], a_rk[u]], axis=1), pqv[u]))
    mn = each(lambda u: _dot_tn(jnp.concatenate([bw_s[u], kw_s[u]], axis=0), pqv[u]))
    g_mat = each(lambda u: (r_s[u].astype(F32) + gy[u][:, :c2]).astype(BF16))
    y0 = each(lambda u: gy[u][:, c2:])
    n0 = each(lambda u: mn[u][:, c2:])

    def decay_plus(u):
        c, p = units[u]
        w_end = wend_ref[0, c, :, p * LANES:(p + 1) * LANES]
        return (jnp.where(eye, w_end, 0.0) + mn[u][:, :c2]).astype(BF16)

    m_mat = each(decay_plus)
    state = [st_ref[p] for p in range(n_pairs)]
    for u, (c, p) in enumerate(units):
        s_bf = state[p].astype(BF16)
        ys = _dot(g_mat[u], s_bf) + y0[u]
        state[p] = _dot(m_mat[u], s_bf) + n0[u]
        y_ref[0, c * chunk:(c + 1) * chunk, p * LANES:(p + 1) * LANES] = (ys[:chunk] + ys[chunk:]).astype(y_ref.dtype)
    for p in range(n_pairs):
        st_ref[p] = state[p]


def _even_out_mix(y_ref, bonus_ref, g_ref, yc_ref, lnw_ref, lnb_ref, hmean_ref, wo_ref, *, cw):
    y = y_ref[...].astype(F32)
    hmean = hmean_ref[...]
    d = y - _dot_bf(y, hmean)
    var = _dot_bf(d * d, hmean)
    yn = d * lax.rsqrt(var + RWKV_GN_EPS) * lnw_ref[...] + lnb_ref[...]
    y_rwkv = (yn + bonus_ref[...].astype(F32)) * g_ref[...].astype(F32)
    return _dot(yc_ref[...], wo_ref[0:cw, :]) + _dot(y_rwkv.astype(BF16), wo_ref[cw:, :])


def _conv_rwkv_mixer(x, gain, w_in, conv_w, mu_shift, w0, w_up, a0, a_up, g_up, k_k, k_a, r_k, ln_w, ln_b, w_out):
    b, t, d = x.shape
    cw = conv_w.shape[1]
    heads, hd = r_k.shape
    rw = heads * hd
    ranks = (w_up.shape[0], a_up.shape[0], g_up.shape[0])
    assert max(ranks) <= LANES and rw % LANES == 0 and cw == rw and LANES % hd == 0

    base = 3 * cw + 3 * rw
    offs = (base, base + ranks[0], base + ranks[0] + ranks[1])

    def padded_cols(arr):
        blocks = [jnp.pad(arr[..., o:o + n], [(0, 0)] * (arr.ndim - 1) + [(0, LANES - n)])
                  for o, n in zip(offs, ranks)]
        return jnp.concatenate([arr[..., :base]] + blocks, axis=-1)

    w_cat = padded_cols(w_in).astype(BF16)
    mu_cat = padded_cols(jnp.pad(mu_shift.reshape(1, -1), ((0, 0), (3 * cw, 0))))[:, 3 * cw:]

    def pad_rows(m):
        m = jnp.pad(m, ((0, LANES - m.shape[0]), (0, 0)))
        hi = m.astype(BF16)
        return jnp.stack([hi, (m - hi.astype(F32)).astype(BF16)])

    head_id = np.arange(rw) // hd
    hsum = jnp.asarray((head_id[:, None] == head_id[None, :]).astype(np.float32)).astype(BF16)
    tri = jnp.asarray(np.tril(np.ones((RWKV_CHUNK, RWKV_CHUNK), np.float32))).astype(BF16)
    row = lambda vec: vec.reshape(1, -1)

    tm = _tile(t, TOKEN_TILE)
    wide = w_cat.shape[1]
    assert tm % RWKV_CHUNK == 0
    chunks = tm // RWKV_CHUNK
    tok = pl.BlockSpec((1, tm, rw), lambda bi, i: (bi, i, 0))
    seq_bf = jax.ShapeDtypeStruct((b, t, rw), BF16)
    a_sc, r_sc, b_sc, k_sc, v_bf, b_end, k_end, w_end, gate, y_conv, bonus = pl.pallas_call(
        functools.partial(_even_in_kernel, cw=cw, rw=rw, chunk=RWKV_CHUNK),
        out_shape=[seq_bf] * 7 + [jax.ShapeDtypeStruct((b, t // RWKV_CHUNK, 1, rw), F32)] + [seq_bf] * 3,
        grid=(b, t // tm),
        in_specs=[pl.BlockSpec((1, tm, d), lambda bi, i: (bi, i, 0)),
                  pl.BlockSpec((1, HALO, d), lambda bi, i: (bi, jnp.maximum(i * (tm // HALO) - 1, 0), 0)),
                  _resident((1, d)), _resident((d, wide)), _resident((1, wide - 3 * cw)),
                  _resident(conv_w.shape), _resident((1, rw)), _resident((2, LANES, rw)),
                  _resident((1, rw)), _resident((2, LANES, rw)), _resident((2, LANES, rw)),
                  _resident((1, rw)), _resident((1, rw)), _resident((1, rw)), _resident((rw, rw)),
                  _resident((RWKV_CHUNK, RWKV_CHUNK))],
        out_specs=[tok] * 7 + [pl.BlockSpec((1, chunks, 1, rw), lambda bi, i: (bi, i, 0, 0))] + [tok] * 3,
        scratch_shapes=[pltpu.VMEM((tm + HALO, d), BF16), pltpu.VMEM((tm + HALO, max(cw, rw)), F32)],
        compiler_params=_params("parallel", "parallel"),
        name="even_in",
    )(x, x, row(gain), w_cat, mu_cat, conv_w, row(w0), pad_rows(w_up), row(a0), pad_rows(a_up),
      pad_rows(g_up), row(k_k), row(k_a), row(r_k), hsum, tri)

    tc = _tile(t, RWKV_BLOCK)
    blk = pl.BlockSpec((1, tc, rw), lambda bi, j: (bi, j, 0))
    y = pl.pallas_call(
        functools.partial(_rwkv_chunk_kernel, chunk=RWKV_CHUNK),
        out_shape=seq_bf,
        grid=(b, t // tc),
        in_specs=[blk] * 7 + [pl.BlockSpec((1, tc // RWKV_CHUNK, 1, rw), lambda bi, j: (bi, j, 0, 0))],
        out_specs=blk,
        scratch_shapes=[pltpu.VMEM((rw // LANES, LANES, LANES), F32)],
        compiler_params=_params("parallel", "arbitrary"),
        name="rwkv_chunk",
    )(a_sc, r_sc, b_sc, k_sc, v_bf, b_end, k_end, w_end)

    m = b * t
    tmo = _tile(m, TOKEN_TILE)
    flat = lambda z: z.reshape(m, -1)
    tokf = pl.BlockSpec((tmo, rw), lambda i: (i, 0))
    operands = [flat(y), flat(bonus), flat(gate), flat(y_conv), row(ln_w), row(ln_b),
                hsum / hd,
                w_out.astype(BF16)]
    specs = [tokf, tokf, tokf, tokf, _resident((1, rw)), _resident((1, rw)), _resident((rw, rw)),
             _resident((cw + rw, d))]
    return functools.partial(_even_out_mix, cw=cw), operands, specs


def _rope_table_kernel(pos_ref, freq_ref, sel_ref, cs_ref):
    ang = pos_ref[...].astype(F32) * freq_ref[...]
    cos, sin = _split(jnp.cos(ang), 3), _split(jnp.sin(ang), 3)
    tr = ang.shape[0]
    for g in range(sel_ref.shape[0]):
        sel = sel_ref[g]
        cs_ref[g * tr:(g + 1) * tr, 0:LANES] = sum(_dot(piece, sel) for piece in cos)
        cs_ref[g * tr:(g + 1) * tr, LANES:2 * LANES] = sum(_dot(piece, sel) for piece in sin)


def _rope_tables(positions, rope_dim):
    b, t = positions.shape
    half = rope_dim // 2
    assert LANES % half == 0
    per_row = LANES // half
    tr = _tile(b * t // per_row, 256)
    tiles = b * t // (per_row * tr)
    inv_freq = ROPE_THETA ** (-jnp.arange(0, rope_dim, 2, dtype=F32) / rope_dim)
    pos_rep = jnp.repeat(positions.reshape(tiles, per_row, tr).transpose(0, 2, 1), half, axis=2).reshape(tiles * tr, LANES)
    lane = np.arange(LANES)
    sel = jnp.asarray(np.stack([(lane[:, None] == g * half + lane[None, :] % half) for g in range(per_row)])
                      .astype(np.float32)).astype(BF16)
    cs = pl.pallas_call(
        _rope_table_kernel,
        out_shape=jax.ShapeDtypeStruct((b * t, 2 * LANES), F32),
        grid=(tiles,),
        in_specs=[pl.BlockSpec((tr, LANES), lambda i: (i, 0)), _resident((1, LANES)), _resident(sel.shape)],
        out_specs=pl.BlockSpec((per_row * tr, 2 * LANES), lambda i: (i, 0)),
        compiler_params=_params("parallel"),
        name="rope_table",
    )(pos_rep, jnp.tile(inv_freq, per_row).reshape(1, LANES), sel)
    return cs.reshape(b, t, 2 * LANES)


def _mla_in_kernel(x_ref, cs_ref, gain_ref, win_ref, qa_ref, kva_ref, wq_ref, wkv_ref,
                   gqn_ref, gqa_ref, gqb_ref, gkn_ref, gka_ref, gkb_ref, ones_ref, q_o, k_o, v_o,
                   *, heads, q_rank, kv_rank, nope, rope, scale):
    qk_dim = nope + rope
    tm = x_ref.shape[1]
    h = _rms(x_ref[0], gain_ref[...]).astype(BF16)
    p = _dot(h, win_ref[...])
    c_q = _rms(p[:, :q_rank], qa_ref[...]).astype(BF16)
    c_kv = _rms(p[:, q_rank:q_rank + kv_rank], kva_ref[...]).astype(BF16)
    base = q_rank + kv_rank
    k_a, k_b = p[:, base:base + LANES], p[:, base + LANES:base + 2 * LANES]
    q = _dot(c_q, wq_ref[...])
    kv = _dot(c_kv, wkv_ref[...])
    cos, sin = cs_ref[0, :, 0:LANES], cs_ref[0, :, LANES:2 * LANES]
    lane = lax.broadcasted_iota(jnp.int32, (tm, LANES), 1)
    first = lane < rope
    ones = ones_ref[...]
    one_hot = jnp.where(lane == 0, 1.0, 0.0).astype(BF16)
    pairs = heads // 2
    rope_a = [q[:, (heads + j) * LANES:(heads + j + 1) * LANES] for j in range(pairs)]
    rope_b = [q[:, (heads + pairs + j) * LANES:(heads + pairs + j + 1) * LANES] for j in range(pairs)]
    q_rot = [rope_a[j] * gqa_ref[...] * cos + rope_b[j] * gqb_ref[...] * sin for j in range(pairs)]
    k_rot = k_a * gka_ref[...] * cos + k_b * gkb_ref[...] * sin
    k_rot_sq = jnp.where(first, k_a * k_a, 0.0)
    mine = [first if hd % 2 == 0 else jnp.logical_not(first) for hd in range(heads)]
    q_n = [q[:, hd * LANES:(hd + 1) * LANES] for hd in range(heads)]
    k_n = [kv[:, hd * 2 * LANES:hd * 2 * LANES + nope] for hd in range(heads)]
    sq_q = [q_n[hd] * q_n[hd] + jnp.where(mine[hd], rope_a[hd // 2] * rope_a[hd // 2], 0.0) for hd in range(heads)]
    sq_k = [k_n[hd] * k_n[hd] + k_rot_sq for hd in range(heads)]
    pair_sum = lambda sq, j: _dot(jnp.concatenate([sq[2 * j], sq[2 * j + 1]], axis=1).astype(BF16), ones)
    ss_q = [pair_sum(sq_q, j) for j in range(pairs)]
    ss_k = [pair_sum(sq_k, j) for j in range(pairs)]
    for hd in range(heads):
        half = slice((hd % 2) * LANES, (hd % 2 + 1) * LANES)
        rs_q = lax.rsqrt(ss_q[hd // 2][:, half] / qk_dim + NORM_EPS) * scale
        rs_k = lax.rsqrt(ss_k[hd // 2][:, half] / qk_dim + NORM_EPS)
        q_o[0, hd, :, 0:nope] = (q_n[hd] * rs_q * gqn_ref[...]).astype(BF16)
        q_o[0, hd, :, nope:nope + LANES] = (jnp.where(mine[hd], q_rot[hd // 2], 0.0) * rs_q).astype(BF16)
        k_o[0, hd, :, 0:nope] = (k_n[hd] * rs_k * gkn_ref[...]).astype(BF16)
        k_o[0, hd, :, nope:nope + LANES] = (k_rot * rs_k).astype(BF16)
        v_o[0, hd, :, 0:LANES] = kv[:, hd * 2 * LANES + nope:(hd + 1) * 2 * LANES].astype(BF16)
        v_o[0, hd, :, LANES:2 * LANES] = one_hot


def _flash_kernel(q_ref, k_ref, v_ref, o_ref, sa_ref, sb_ref, m_ref, acc_ref):
    i = pl.program_id(2)
    nh, tq = q_ref.shape[1], q_ref.shape[2]
    vd = o_ref.shape[2] // nh
    heads = range(nh)

    def block(j):
        return pl.ds(pl.multiple_of(j * tq, tq), tq)

    def logits(h, j):
        return _dot_nt(q_ref[0, h], k_ref[0, h, block(j), :])

    def consume(h, j, s):
        m = m_ref[h]
        m_new = jnp.maximum(m, jnp.max(s, axis=-1, keepdims=True))
        pr = jnp.exp2(s - m_new).astype(BF16)
        acc_ref[h] = jnp.exp2(m - m_new) * acc_ref[h] + _dot(pr, v_ref[0, h, block(j), :])
        m_ref[h] = m_new

    def causal(s):
        keep = lax.broadcasted_iota(jnp.int32, s.shape, 1) <= lax.broadcasted_iota(jnp.int32, s.shape, 0)
        return jnp.where(keep, s, -jnp.inf)

    m_ref[...] = jnp.full(m_ref.shape, -jnp.inf, F32)
    acc_ref[...] = jnp.zeros(acc_ref.shape, F32)
    for h in heads:
        sa_ref[h] = logits(h, 0)

    def pair(t, carry):
        j = 2 * t
        for h in heads:
            sb_ref[h] = logits(h, j + 1)
        for h in heads:
            consume(h, j, sa_ref[h])
        for h in heads:
            sa_ref[h] = logits(h, j + 2)
        for h in heads:
            consume(h, j + 1, sb_ref[h])
        return carry

    lax.fori_loop(0, i // 2, pair, 0)

    @pl.when(i % 2 == 0)
    def _():
        for h in heads:
            consume(h, i, causal(sa_ref[h]))

    @pl.when(i % 2 == 1)
    def _():
        for h in heads:
            sb_ref[h] = logits(h, i)
        for h in heads:
            consume(h, i - 1, sa_ref[h])
        for h in heads:
            consume(h, i, causal(sb_ref[h]))

    for h in heads:
        acc = acc_ref[h]
        o_ref[0, :, h * vd:(h + 1) * vd] = (acc[:, :vd] / acc[:, vd:vd + 1]).astype(o_ref.dtype)


def _mla_out_mix(o_ref, wo_ref):
    return _dot(o_ref[...], wo_ref[...])


def _mla_mixer(x, cs, gain, w_in, q_a_norm, kv_a_norm, w_q_up, w_kv_up, q_norm, k_norm, w_out):
    b, t, d = x.shape
    q_rank, kv_rank, qk_dim = q_a_norm.shape[0], kv_a_norm.shape[0], q_norm.shape[0]
    rope = w_in.shape[1] - q_rank - kv_rank
    nope = qk_dim - rope
    heads = w_q_up.shape[1] // qk_dim
    v_dim = w_kv_up.shape[1] // heads - nope
    half = rope // 2
    blk = nope + LANES
    assert nope == LANES and v_dim == LANES and 2 * rope == LANES and heads % 2 == 0
    assert q_rank % LANES == 0 and kv_rank % LANES == 0

    def rot_cols(w):
        return jnp.concatenate([-w[..., half:], w[..., :half]], axis=-1)

    def rot_gain(g):
        return jnp.concatenate([g[half:], g[:half]])

    twice = lambda z: jnp.concatenate([z, z], axis=-1)
    w_pe = w_in[:, q_rank + kv_rank:]
    w_in_ext = jnp.concatenate([w_in[:, :q_rank + kv_rank], twice(w_pe), twice(rot_cols(w_pe))], axis=1).astype(BF16)
    wq = w_q_up.reshape(q_rank, heads, qk_dim)
    wq_ext = jnp.concatenate([wq[..., :nope].reshape(q_rank, heads * nope),
                              wq[..., nope:].reshape(q_rank, heads * rope),
                              rot_cols(wq[..., nope:]).reshape(q_rank, heads * rope)], axis=1).astype(BF16)
    row = lambda vec: vec.reshape(1, -1)
    lane_block = np.arange(2 * LANES) // LANES
    ones = jnp.asarray((lane_block[:, None] == lane_block[None, :]).astype(np.float32)).astype(BF16)

    tm = _tile(t, TOKEN_TILE)
    head_blk = pl.BlockSpec((1, heads, tm, blk), lambda bi, i: (bi, 0, i, 0))
    head_arr = jax.ShapeDtypeStruct((b, heads, t, blk), BF16)
    lane_row = _resident((1, LANES))
    q, k, v = pl.pallas_call(
        functools.partial(_mla_in_kernel, heads=heads, q_rank=q_rank, kv_rank=kv_rank, nope=nope, rope=rope,
                          scale=qk_dim ** -0.5 * math.log2(math.e)),
        out_shape=[head_arr] * 3,
        grid=(b, t // tm),
        in_specs=[pl.BlockSpec((1, tm, d), lambda bi, i: (bi, i, 0)),
                  pl.BlockSpec((1, tm, 2 * LANES), lambda bi, i: (bi, i, 0)),
                  _resident((1, d)), _resident(w_in_ext.shape), _resident((1, q_rank)), _resident((1, kv_rank)),
                  _resident(wq_ext.shape), _resident(w_kv_up.shape),
                  lane_row, lane_row, lane_row, lane_row, lane_row, lane_row, _resident(ones.shape)],
        out_specs=[head_blk] * 3,
        compiler_params=_params("parallel", "parallel"),
        name="mla_in",
    )(x, cs, row(gain), w_in_ext, row(q_a_norm), row(kv_a_norm), wq_ext, w_kv_up.astype(BF16),
      row(q_norm[:nope]), row(twice(q_norm[nope:])), row(twice(rot_gain(q_norm[nope:]))),
      row(k_norm[:nope]), row(twice(k_norm[nope:])), row(twice(rot_gain(k_norm[nope:]))), ones)

    tq = _tile(t, ATTN_TILE)
    nh = ATTN_HEADS_PER_STEP
    assert heads % nh == 0
    full = pl.BlockSpec((1, nh, t, blk), lambda bi, hi, i: (bi, hi, 0, 0))
    o = pl.pallas_call(
        _flash_kernel,
        out_shape=jax.ShapeDtypeStruct((b, t, heads * v_dim), BF16),
        grid=(b, heads // nh, t // tq),
        in_specs=[pl.BlockSpec((1, nh, tq, blk), lambda bi, hi, i: (bi, hi, i, 0)), full, full],
        out_specs=pl.BlockSpec((1, tq, nh * v_dim), lambda bi, hi, i: (bi, i, hi)),
        scratch_shapes=[pltpu.VMEM((nh, tq, tq), F32), pltpu.VMEM((nh, tq, tq), F32), pltpu.VMEM((nh, tq, 1), F32),
                        pltpu.VMEM((nh, tq, blk), F32)],
        compiler_params=_params("parallel", "parallel", "arbitrary"),
        name="flash",
    )(q, k, v)

    m = b * t
    tmo = _tile(m, TOKEN_TILE)
    operands = [o.reshape(m, heads * v_dim), w_out.astype(BF16)]
    specs = [pl.BlockSpec((tmo, heads * v_dim), lambda i: (i, 0)), _resident(w_out.shape)]
    return _mla_out_mix, operands, specs


def kernel(x, positions, norm_gains, ffn_w_gate, ffn_w_up, ffn_w_down, even_w_in, even_conv_w, even_mu_shift, rwkv_w0, rwkv_w_up, rwkv_a0, rwkv_a_up, rwkv_g_up, rwkv_k_k, rwkv_k_a, rwkv_r_k, rwkv_ln_w, rwkv_ln_b, even_w_out, odd_w_in, mla_q_a_norm, mla_kv_a_norm, mla_w_q_up, mla_w_kv_up, mla_q_norm, mla_k_norm, odd_w_out):
    b, t, d = x.shape
    depth = norm_gains.shape[0]
    rope_dim = odd_w_in.shape[2] - mla_q_a_norm.shape[1] - mla_kv_a_norm.shape[1]
    cs = _rope_tables(positions, rope_dim)

    def ffn(x, layer, which, gain, mix=None):
        y = _ffn(x.reshape(b * t, d), gain, ffn_w_gate[layer, which].astype(BF16),
                 ffn_w_up[layer, which].astype(BF16), ffn_w_down[layer, which].astype(BF16), mix)
        return y.reshape(b, t, d)

    for layer in range(depth):
        g = norm_gains[layer]
        x = ffn(x, layer, 0, g[0])
        if layer % 2 == 0:
            i = layer // 2
            mix = _conv_rwkv_mixer(x, g[1], even_w_in[i], even_conv_w[i], even_mu_shift[i], rwkv_w0[i],
                                   rwkv_w_up[i], rwkv_a0[i], rwkv_a_up[i], rwkv_g_up[i], rwkv_k_k[i],
                                   rwkv_k_a[i], rwkv_r_k[i], rwkv_ln_w[i], rwkv_ln_b[i], even_w_out[i])
        else:
            j = layer // 2
            mix = _mla_mixer(x, cs, g[1], odd_w_in[j], mla_q_a_norm[j], mla_kv_a_norm[j], mla_w_q_up[j],
                             mla_w_kv_up[j], mla_q_norm[j], mla_k_norm[j], odd_w_out[j])
        x = ffn(x, layer, 1, g[2], mix)
    return x
```

```python
import functools
import math

import numpy as np
import jax
import jax.numpy as jnp
from jax import lax
from jax.experimental import pallas as pl
from jax.experimental.pallas import tpu as pltpu

F32 = jnp.float32
BF16 = jnp.bfloat16

NORM_EPS = 1e-6
RWKV_GN_EPS = 64e-5
ROPE_THETA = 10000.0
KK_NORM_FLOOR = 1e-24

LANES = 128
BF16_ROWS = 16
HALO = BF16_ROWS
VMEM_LIMIT = 56 * 1024 * 1024
RWKV_CHUNK = 64
RWKV_BLOCK = 256
TOKEN_TILE = 512
ATTN_TILE = 512
ATTN_HEADS_PER_STEP = 2


def _dot(a, b):
    return jnp.dot(a, b, preferred_element_type=F32)


def _dot_bf(a, b):
    return jnp.dot(a.astype(BF16), b.astype(BF16), preferred_element_type=F32)


def _dot_nt(a, b):
    return lax.dot_general(a, b, (((1,), (1,)), ((), ())), preferred_element_type=F32)


def _dot_tn(a, b):
    return lax.dot_general(a, b, (((0,), (0,)), ((), ())), preferred_element_type=F32)


def _split(x, parts):
    pieces = []
    for _ in range(parts):
        piece = x.astype(BF16)
        pieces.append(piece)
        x = x - piece.astype(F32)
    return pieces


def _dot_split_rhs(w, x, parts):
    return sum(_dot(w, piece) for piece in _split(x, parts))


def _dot_hi_lo(x, w_ref):
    hi, lo = _split(x, 2)
    return _dot(hi, w_ref[0]) + (_dot(hi, w_ref[1]) + _dot(lo, w_ref[0]))


def _rms(x, gain, eps=NORM_EPS):
    return x * lax.rsqrt(jnp.mean(x * x, axis=-1, keepdims=True) + eps) * gain


def _resident(shape):
    nd = len(shape)
    return pl.BlockSpec(shape, lambda *_: (0,) * nd, pipeline_mode=pl.Buffered(1))


def _params(*sem):
    return pltpu.CompilerParams(dimension_semantics=sem, vmem_limit_bytes=VMEM_LIMIT)


def _tile(n, want):
    t = min(n, want)
    assert n % t == 0, (n, t)
    return t


def _ffn_kernel(*refs, tf, n_mix, mix_fn):
    x_ref, mix_refs = refs[0], refs[1:1 + n_mix]
    gain_ref, wg_ref, wu_ref, wd_ref, o_ref, act_ref = refs[1 + n_mix:]
    x = x_ref[...]
    if mix_fn is not None:
        x = x + mix_fn(*mix_refs)
    h = _rms(x, gain_ref[...]).astype(BF16)
    for f in range(wg_ref.shape[1] // tf):
        cols = slice(f * tf, (f + 1) * tf)
        g = _dot(h, wg_ref[:, cols])
        u = _dot(h, wu_ref[:, cols])
        act_ref[:, cols] = (g * jax.nn.sigmoid(g) * u).astype(BF16)
    o_ref[...] = x + 0.5 * _dot(act_ref[...], wd_ref[...])


def _ffn(x2d, gain, wg, wu, wd, mix=None):
    m, d = x2d.shape
    f = wg.shape[1]
    tm = _tile(m, TOKEN_TILE)
    tf = 256 if f % 256 == 0 else LANES
    mix_fn, mix_ops, mix_specs = mix if mix is not None else (None, [], [])
    return pl.pallas_call(
        functools.partial(_ffn_kernel, tf=tf, n_mix=len(mix_ops), mix_fn=mix_fn),
        out_shape=jax.ShapeDtypeStruct((m, d), F32),
        grid=(m // tm,),
        in_specs=[pl.BlockSpec((tm, d), lambda i: (i, 0))] + list(mix_specs)
                 + [_resident((1, d)), _resident((d, f)), _resident((d, f)), _resident((f, d))],
        out_specs=pl.BlockSpec((tm, d), lambda i: (i, 0)),
        scratch_shapes=[pltpu.VMEM((tm, f), BF16)],
        compiler_params=_params("parallel"),
        name="ffn",
    )(x2d, *mix_ops, gain.reshape(1, d), wg, wu, wd)


def _even_in_kernel(x_ref, xh_ref, gain_ref, w_ref, mu_ref, convw_ref, w0_ref, wup_ref, a0_ref, aup_ref,
                    gup_ref, kk_ref, ka_ref, rk_ref, hsum_ref, tri_ref,
                    a_o, r_o, b_o, k_o, v_o, bw_o, kw_o, wend_o, g_o, yc_o, bonus_o,
                    h_s, p_s, *, cw, rw, chunk):
    tm = x_ref.shape[1]
    gain = gain_ref[...]
    xh = jnp.where(pl.program_id(1) > 0, xh_ref[0], 0.0)
    h_s[0:HALO, :] = _rms(xh, gain).astype(BF16)
    h_s[HALO:, :] = _rms(x_ref[0], gain).astype(BF16)

    def proj(c0, n):
        return _dot(h_s[...], w_ref[:, c0:c0 + n])

    def staged(back, n):
        return p_s[pl.ds(HALO - back, tm), 0:n]

    gate_b = proj(0, cw)[HALO:]
    p_s[:, 0:cw] = proj(cw, cw) * proj(2 * cw, cw)
    cwt = convw_ref[...]
    yc_o[0] = (gate_b * (staged(2, cw) * cwt[0:1] + staged(1, cw) * cwt[1:2] + staged(0, cw) * cwt[2:3])).astype(BF16)

    def shifted(c0, n):
        p_s[:, 0:n] = proj(c0, n)
        cur = staged(0, n)
        return cur + (staged(1, n) - cur) * mu_ref[:, c0 - 3 * cw:c0 - 3 * cw + n]

    base = 3 * cw
    r = shifted(base, rw)
    k = shifted(base + rw, rw)
    v = shifted(base + 2 * rw, rw)
    lora = shifted(base + 3 * rw, 3 * LANES)
    dw, da, dg = lora[:, 0:LANES], lora[:, LANES:2 * LANES], lora[:, 2 * LANES:3 * LANES]

    zw = w0_ref[...] + _dot_hi_lo(jnp.tanh(dw), wup_ref)
    lw = -math.exp(-0.5) * jax.nn.sigmoid(zw)
    iclr = jax.nn.sigmoid(a0_ref[...] + _dot_bf(da, aup_ref[0]))
    g_o[0] = _dot_bf(jax.nn.sigmoid(dg), gup_ref[0]).astype(BF16)
    hsum = hsum_ref[...]
    kkr = k * kk_ref[...]
    kk = kkr * lax.rsqrt(jnp.maximum(_dot_bf(kkr * kkr, hsum), KK_NORM_FLOOR))
    kh = k * (1.0 + (iclr - 1.0) * ka_ref[...])
    bonus_o[0] = (_dot_bf(r * kh * rk_ref[...], hsum) * v).astype(BF16)
    b_vec = kk * iclr
    v_o[0] = v.astype(BF16)

    tri = tri_ref[...]
    for c in range(tm // chunk):
        rows = slice(c * chunk, (c + 1) * chunk)
        lw_c = lw[rows]
        cum = _dot_split_rhs(tri, lw_c, 2)
        tot = cum[chunk - 1:chunk, :]
        grow = jnp.exp(-cum)
        to_end = jnp.exp(tot - cum)
        a_o[0, rows, :] = (-kk[rows] * jnp.exp(cum - lw_c)).astype(BF16)
        r_o[0, rows, :] = (r[rows] * jnp.exp(cum)).astype(BF16)
        b_o[0, rows, :] = (b_vec[rows] * grow).astype(BF16)
        k_o[0, rows, :] = (kh[rows] * grow).astype(BF16)
        bw_o[0, rows, :] = (b_vec[rows] * to_end).astype(BF16)
        kw_o[0, rows, :] = (kh[rows] * to_end).astype(BF16)
        wend_o[0, c] = jnp.exp(tot)


def _rwkv_chunk_kernel(a_ref, r_ref, b_ref, k_ref, v_ref, bw_ref, kw_ref, wend_ref, y_ref, st_ref, *, chunk):
    @pl.when(pl.program_id(1) == 0)
    def _():
        st_ref[...] = jnp.zeros_like(st_ref)

    tc, width = a_ref.shape[1], a_ref.shape[2]
    c2 = 2 * chunk
    assert c2 == LANES
    n_pairs = width // LANES
    units = [(c, p) for c in range(tc // chunk) for p in range(n_pairs)]
    row = lax.broadcasted_iota(jnp.int32, (c2, c2), 0)
    col = lax.broadcasted_iota(jnp.int32, (c2, c2), 1)
    same = (row // chunk) == (col // chunk)
    strict = same & (col < row)
    incl = same & (col <= row)
    eye = row == col
    head0 = lax.broadcasted_iota(jnp.int32, (chunk, LANES), 1) < chunk

    def stacked(ref):
        out = []
        for c, p in units:
            x = ref[0, c * chunk:(c + 1) * chunk, p * LANES:(p + 1) * LANES]
            zero = jnp.zeros_like(x)
            out.append(jnp.concatenate([jnp.where(head0, x, zero), jnp.where(head0, zero, x)], axis=0))
        return out

    a_s, r_s, b_s, k_s, v_s, bw_s, kw_s = (stacked(ref) for ref in (a_ref, r_ref, b_ref, k_ref, v_ref, bw_ref, kw_ref))
    each = lambda fn: [fn(u) for u in range(len(units))]
    att = each(lambda u: _dot_nt(jnp.concatenate([a_s[u], r_s[u]], axis=0), jnp.concatenate([b_s[u], k_s[u]], axis=0)))
    power = each(lambda u: jnp.where(strict, att[u][:c2, :c2], 0.0).astype(BF16))
    a_ak = each(lambda u: jnp.where(strict, att[u][:c2, c2:], 0.0).astype(BF16))
    a_rb = each(lambda u: jnp.where(incl, att[u][c2:, :c2], 0.0).astype(BF16))
    a_rk = each(lambda u: jnp.where(incl, att[u][c2:, c2:], 0.0).astype(BF16))
    av = each(lambda u: _dot(a_ak[u], v_s[u]).astype(BF16))
    tinv = each(lambda u: jnp.where(eye, 1.0, 0.0) + power[u].astype(F32))
    for _ in range(int(math.log2(chunk)) - 1):
        power = each(lambda u: _dot(power[u], power[u]).astype(BF16))
        tinv = each(lambda u: tinv[u] + _dot(tinv[u].astype(BF16), power[u]))
    pq = each(lambda u: _dot(tinv[u].astype(BF16), jnp.concatenate([a_s[u], av[u]], axis=1)).astype(BF16))
    zero = jnp.zeros((c2, c2), BF16)
    pqv = each(lambda u: jnp.concatenate([pq[u], jnp.concatenate([zero, v_s[u]], axis=1)], axis=0))
    gy = each(lambda u: _dot(jnp.concatenate([a_rb[u], a_rk[u]], axis=1), pqv[u]))
    mn = each(lambda u: _dot_tn(jnp.concatenate([bw_s[u], kw_s[u]], axis=0), pqv[u]))
    g_mat = each(lambda u: (r_s[u].astype(F32) + gy[u][:, :c2]).astype(BF16))
    y0 = each(lambda u: gy[u][:, c2:])
    n0 = each(lambda u: mn[u][:, c2:])

    def decay_plus(u):
        c, p = units[u]
        w_end = wend_ref[0, c, :, p * LANES:(p + 1) * LANES]
        return (jnp.where(eye, w_end, 0.0) + mn[u][:, :c2]).astype(BF16)

    m_mat = each(decay_plus)
    state = [st_ref[p] for p in range(n_pairs)]
    for u, (c, p) in enumerate(units):
        s_bf = state[p].astype(BF16)
        ys = _dot(g_mat[u], s_bf) + y0[u]
        state[p] = _dot(m_mat[u], s_bf) + n0[u]
        y_ref[0, c * chunk:(c + 1) * chunk, p * LANES:(p + 1) * LANES] = (ys[:chunk] + ys[chunk:]).astype(y_ref.dtype)
    for p in range(n_pairs):
        st_ref[p] = state[p]


def _even_out_mix(y_ref, bonus_ref, g_ref, yc_ref, lnw_ref, lnb_ref, hmean_ref, wo_ref, *, cw):
    y = y_ref[...].astype(F32)
    hmean = hmean_ref[...]
    d = y - _dot_bf(y, hmean)
    var = _dot_bf(d * d, hmean)
    yn = d * lax.rsqrt(var + RWKV_GN_EPS) * lnw_ref[...] + lnb_ref[...]
    y_rwkv = (yn + bonus_ref[...].astype(F32)) * g_ref[...].astype(F32)
    return _dot(yc_ref[...], wo_ref[0:cw, :]) + _dot(y_rwkv.astype(BF16), wo_ref[cw:, :])


def _conv_rwkv_mixer(x, gain, w_in, conv_w, mu_shift, w0, w_up, a0, a_up, g_up, k_k, k_a, r_k, ln_w, ln_b, w_out):
    b, t, d = x.shape
    cw = conv_w.shape[1]
    heads, hd = r_k.shape
    rw = heads * hd
    ranks = (w_up.shape[0], a_up.shape[0], g_up.shape[0])
    assert max(ranks) <= LANES and rw % LANES == 0 and cw == rw and LANES % hd == 0

    base = 3 * cw + 3 * rw
    offs = (base, base + ranks[0], base + ranks[0] + ranks[1])

    def padded_cols(arr):
        blocks = [jnp.pad(arr[..., o:o + n], [(0, 0)] * (arr.ndim - 1) + [(0, LANES - n)])
                  for o, n in zip(offs, ranks)]
        return jnp.concatenate([arr[..., :base]] + blocks, axis=-1)

    w_cat = padded_cols(w_in).astype(BF16)
    mu_cat = padded_cols(jnp.pad(mu_shift.reshape(1, -1), ((0, 0), (3 * cw, 0))))[:, 3 * cw:]

    def pad_rows(m):
        m = jnp.pad(m, ((0, LANES - m.shape[0]), (0, 0)))
        hi = m.astype(BF16)
        return jnp.stack([hi, (m - hi.astype(F32)).astype(BF16)])

    head_id = np.arange(rw) // hd
    hsum = jnp.asarray((head_id[:, None] == head_id[None, :]).astype(np.float32)).astype(BF16)
    tri = jnp.asarray(np.tril(np.ones((RWKV_CHUNK, RWKV_CHUNK), np.float32))).astype(BF16)
    row = lambda vec: vec.reshape(1, -1)

    tm = _tile(t, TOKEN_TILE)
    wide = w_cat.shape[1]
    assert tm % RWKV_CHUNK == 0
    chunks = tm // RWKV_CHUNK
    tok = pl.BlockSpec((1, tm, rw), lambda bi, i: (bi, i, 0))
    seq_bf = jax.ShapeDtypeStruct((b, t, rw), BF16)
    a_sc, r_sc, b_sc, k_sc, v_bf, b_end, k_end, w_end, gate, y_conv, bonus = pl.pallas_call(
        functools.partial(_even_in_kernel, cw=cw, rw=rw, chunk=RWKV_CHUNK),
        out_shape=[seq_bf] * 7 + [jax.ShapeDtypeStruct((b, t // RWKV_CHUNK, 1, rw), F32)] + [seq_bf] * 3,
        grid=(b, t // tm),
        in_specs=[pl.BlockSpec((1, tm, d), lambda bi, i: (bi, i, 0)),
                  pl.BlockSpec((1, HALO, d), lambda bi, i: (bi, jnp.maximum(i * (tm // HALO) - 1, 0), 0)),
                  _resident((1, d)), _resident((d, wide)), _resident((1, wide - 3 * cw)),
                  _resident(conv_w.shape), _resident((1, rw)), _resident((2, LANES, rw)),
                  _resident((1, rw)), _resident((2, LANES, rw)), _resident((2, LANES, rw)),
                  _resident((1, rw)), _resident((1, rw)), _resident((1, rw)), _resident((rw, rw)),
                  _resident((RWKV_CHUNK, RWKV_CHUNK))],
        out_specs=[tok] * 7 + [pl.BlockSpec((1, chunks, 1, rw), lambda bi, i: (bi, i, 0, 0))] + [tok] * 3,
        scratch_shapes=[pltpu.VMEM((tm + HALO, d), BF16), pltpu.VMEM((tm + HALO, max(cw, rw)), F32)],
        compiler_params=_params("parallel", "parallel"),
        name="even_in",
    )(x, x, row(gain), w_cat, mu_cat, conv_w, row(w0), pad_rows(w_up), row(a0), pad_rows(a_up),
      pad_rows(g_up), row(k_k), row(k_a), row(r_k), hsum, tri)

    tc = _tile(t, RWKV_BLOCK)
    blk = pl.BlockSpec((1, tc, rw), lambda bi, j: (bi, j, 0))
    y = pl.pallas_call(
        functools.partial(_rwkv_chunk_kernel, chunk=RWKV_CHUNK),
        out_shape=seq_bf,
        grid=(b, t // tc),
        in_specs=[blk] * 7 + [pl.BlockSpec((1, tc // RWKV_CHUNK, 1, rw), lambda bi, j: (bi, j, 0, 0))],
        out_specs=blk,
        scratch_shapes=[pltpu.VMEM((rw // LANES, LANES, LANES), F32)],
        compiler_params=_params("parallel", "arbitrary"),
        name="rwkv_chunk",
    )(a_sc, r_sc, b_sc, k_sc, v_bf, b_end, k_end, w_end)

    m = b * t
    tmo = _tile(m, TOKEN_TILE)
    flat = lambda z: z.reshape(m, -1)
    tokf = pl.BlockSpec((tmo, rw), lambda i: (i, 0))
    operands = [flat(y), flat(bonus), flat(gate), flat(y_conv), row(ln_w), row(ln_b),
                hsum / hd,
                w_out.astype(BF16)]
    specs = [tokf, tokf, tokf, tokf, _resident((1, rw)), _resident((1, rw)), _resident((rw, rw)),
             _resident((cw + rw, d))]
    return functools.partial(_even_out_mix, cw=cw), operands, specs


def _rope_table_kernel(pos_ref, freq_ref, sel_ref, cs_ref):
    ang = pos_ref[...].astype(F32) * freq_ref[...]
    cos, sin = _split(jnp.cos(ang), 3), _split(jnp.sin(ang), 3)
    tr = ang.shape[0]
    for g in range(sel_ref.shape[0]):
        sel = sel_ref[g]
        cs_ref[g * tr:(g + 1) * tr, 0:LANES] = sum(_dot(piece, sel) for piece in cos)
        cs_ref[g * tr:(g + 1) * tr, LANES:2 * LANES] = sum(_dot(piece, sel) for piece in sin)


def _rope_tables(positions, rope_dim):
    b, t = positions.shape
    half = rope_dim // 2
    assert LANES % half == 0
    per_row = LANES // half
    tr = _tile(b * t // per_row, 256)
    tiles = b * t // (per_row * tr)
    inv_freq = ROPE_THETA ** (-jnp.arange(0, rope_dim, 2, dtype=F32) / rope_dim)
    pos_rep = jnp.repeat(positions.reshape(tiles, per_row, tr).transpose(0, 2, 1), half, axis=2).reshape(tiles * tr, LANES)
    lane = np.arange(LANES)
    sel = jnp.asarray(np.stack([(lane[:, None] == g * half + lane[None, :] % half) for g in range(per_row)])
                      .astype(np.float32)).astype(BF16)
    cs = pl.pallas_call(
        _rope_table_kernel,
        out_shape=jax.ShapeDtypeStruct((b * t, 2 * LANES), F32),
        grid=(tiles,),
        in_specs=[pl.BlockSpec((tr, LANES), lambda i: (i, 0)), _resident((1, LANES)), _resident(sel.shape)],
        out_specs=pl.BlockSpec((per_row * tr, 2 * LANES), lambda i: (i, 0)),
        compiler_params=_params("parallel"),
        name="rope_table",
    )(pos_rep, jnp.tile(inv_freq, per_row).reshape(1, LANES), sel)
    return cs.reshape(b, t, 2 * LANES)


def _mla_in_kernel(x_ref, cs_ref, gain_ref, win_ref, qa_ref, kva_ref, wq_ref, wkn_ref, wvt_ref,
                   gqn_ref, gqa_ref, gqb_ref, gkn_ref, gka_ref, gkb_ref, ones_ref, q_o, k_o, v_o,
                   *, heads, q_rank, kv_rank, nope, rope, scale):
    qk_dim = nope + rope
    tm = x_ref.shape[1]
    h = _rms(x_ref[0], gain_ref[...]).astype(BF16)
    p = _dot(h, win_ref[...])
    c_q = _rms(p[:, :q_rank], qa_ref[...]).astype(BF16)
    c_kv = _rms(p[:, q_rank:q_rank + kv_rank], kva_ref[...]).astype(BF16)
    base = q_rank + kv_rank
    k_a, k_b = p[:, base:base + LANES], p[:, base + LANES:base + 2 * LANES]
    q = _dot(c_q, wq_ref[...])
    kn = _dot(c_kv, wkn_ref[...])
    cos, sin = cs_ref[0, :, 0:LANES], cs_ref[0, :, LANES:2 * LANES]
    lane = lax.broadcasted_iota(jnp.int32, (tm, LANES), 1)
    first = lane < rope
    ones = ones_ref[...]
    pairs = heads // 2
    rope_a = [q[:, (heads + j) * LANES:(heads + j + 1) * LANES] for j in range(pairs)]
    rope_b = [q[:, (heads + pairs + j) * LANES:(heads + pairs + j + 1) * LANES] for j in range(pairs)]
    q_rot = [rope_a[j] * gqa_ref[...] * cos + rope_b[j] * gqb_ref[...] * sin for j in range(pairs)]
    k_rot = k_a * gka_ref[...] * cos + k_b * gkb_ref[...] * sin
    k_rot_sq = jnp.where(first, k_a * k_a, 0.0)
    mine = [first if hd % 2 == 0 else jnp.logical_not(first) for hd in range(heads)]
    q_n = [q[:, hd * LANES:(hd + 1) * LANES] for hd in range(heads)]
    k_n = [kn[:, hd * nope:(hd + 1) * nope] for hd in range(heads)]
    sq_q = [q_n[hd] * q_n[hd] + jnp.where(mine[hd], rope_a[hd // 2] * rope_a[hd // 2], 0.0) for hd in range(heads)]
    sq_k = [k_n[hd] * k_n[hd] + k_rot_sq for hd in range(heads)]
    pair_sum = lambda sq, j: _dot(jnp.concatenate([sq[2 * j], sq[2 * j + 1]], axis=1).astype(BF16), ones)
    ss_q = [pair_sum(sq_q, j) for j in range(pairs)]
    ss_k = [pair_sum(sq_k, j) for j in range(pairs)]
    for hd in range(heads):
        half = slice((hd % 2) * LANES, (hd % 2 + 1) * LANES)
        rs_q = lax.rsqrt(ss_q[hd // 2][:, half] / qk_dim + NORM_EPS) * scale
        rs_k = lax.rsqrt(ss_k[hd // 2][:, half] / qk_dim + NORM_EPS)
        q_o[0, hd, :, 0:nope] = (q_n[hd] * rs_q * gqn_ref[...]).astype(BF16)
        q_o[0, hd, :, nope:nope + LANES] = (jnp.where(mine[hd], q_rot[hd // 2], 0.0) * rs_q).astype(BF16)
        k_o[0, hd, :, 0:nope] = (k_n[hd] * rs_k * gkn_ref[...]).astype(BF16)
        k_o[0, hd, :, nope:nope + LANES] = (k_rot * rs_k).astype(BF16)
        v_o[0, hd] = _dot_nt(wvt_ref[hd], c_kv).astype(BF16)


def _flash_kernel(q_ref, k_ref, v_ref, o_ref, sa_ref, sb_ref, m_ref, l_ref, acc_ref):
    i = pl.program_id(2)
    nh, tq = q_ref.shape[1], q_ref.shape[2]
    vd = o_ref.shape[2] // nh
    heads = range(nh)

    def block(j):
        return pl.ds(pl.multiple_of(j * tq, tq), tq)

    def logits(h, j):
        return _dot_nt(k_ref[0, h, block(j), :], q_ref[0, h])

    def consume(h, j, s):
        m = m_ref[h]
        m_new = jnp.maximum(m, jnp.max(s, axis=0, keepdims=True))
        alpha = jnp.exp2(m - m_new)
        pr = jnp.exp2(s - m_new)
        l_ref[h] = alpha * l_ref[h] + jnp.sum(pr, axis=0, keepdims=True)
        acc_ref[h] = alpha * acc_ref[h] + _dot(v_ref[0, h, :, block(j)], pr.astype(BF16))
        m_ref[h] = m_new

    def causal(s):
        keep = lax.broadcasted_iota(jnp.int32, s.shape, 0) <= lax.broadcasted_iota(jnp.int32, s.shape, 1)
        return jnp.where(keep, s, -jnp.inf)

    m_ref[...] = jnp.full(m_ref.shape, -jnp.inf, F32)
    l_ref[...] = jnp.zeros(l_ref.shape, F32)
    acc_ref[...] = jnp.zeros(acc_ref.shape, F32)
    for h in heads:
        sa_ref[h] = logits(h, 0)

    def pair(t, carry):
        j = 2 * t
        for h in heads:
            sb_ref[h] = logits(h, j + 1)
        for h in heads:
            consume(h, j, sa_ref[h])
        for h in heads:
            sa_ref[h] = logits(h, j + 2)
        for h in heads:
            consume(h, j + 1, sb_ref[h])
        return carry

    lax.fori_loop(0, i // 2, pair, 0)

    @pl.when(i % 2 == 0)
    def _():
        for h in heads:
            consume(h, i, causal(sa_ref[h]))

    @pl.when(i % 2 == 1)
    def _():
        for h in heads:
            sb_ref[h] = logits(h, i)
        for h in heads:
            consume(h, i - 1, sa_ref[h])
        for h in heads:
            consume(h, i, causal(sb_ref[h]))

    for h in heads:
        o_ref[0, :, h * vd:(h + 1) * vd] = (acc_ref[h] / l_ref[h]).T.astype(o_ref.dtype)


def _mla_out_mix(o_ref, wo_ref):
    return _dot(o_ref[...], wo_ref[...])


def _mla_mixer(x, cs, gain, w_in, q_a_norm, kv_a_norm, w_q_up, w_kv_up, q_norm, k_norm, w_out):
    b, t, d = x.shape
    q_rank, kv_rank, qk_dim = q_a_norm.shape[0], kv_a_norm.shape[0], q_norm.shape[0]
    rope = w_in.shape[1] - q_rank - kv_rank
    nope = qk_dim - rope
    heads = w_q_up.shape[1] // qk_dim
    v_dim = w_kv_up.shape[1] // heads - nope
    half = rope // 2
    blk = nope + LANES
    assert nope == LANES and v_dim == LANES and 2 * rope == LANES and heads % 2 == 0
    assert q_rank % LANES == 0 and kv_rank % LANES == 0

    def rot_cols(w):
        return jnp.concatenate([-w[..., half:], w[..., :half]], axis=-1)

    def rot_gain(g):
        return jnp.concatenate([g[half:], g[:half]])

    twice = lambda z: jnp.concatenate([z, z], axis=-1)
    w_pe = w_in[:, q_rank + kv_rank:]
    w_in_ext = jnp.concatenate([w_in[:, :q_rank + kv_rank], twice(w_pe), twice(rot_cols(w_pe))], axis=1).astype(BF16)
    wq = w_q_up.reshape(q_rank, heads, qk_dim)
    wq_ext = jnp.concatenate([wq[..., :nope].reshape(q_rank, heads * nope),
                              wq[..., nope:].reshape(q_rank, heads * rope),
                              rot_cols(wq[..., nope:]).reshape(q_rank, heads * rope)], axis=1).astype(BF16)
    wkv = w_kv_up.reshape(kv_rank, heads, nope + v_dim)
    w_kn = wkv[..., :nope].reshape(kv_rank, heads * nope).astype(BF16)
    w_vt = wkv[..., nope:].transpose(1, 2, 0).astype(BF16)
    row = lambda vec: vec.reshape(1, -1)
    lane_block = np.arange(2 * LANES) // LANES
    ones = jnp.asarray((lane_block[:, None] == lane_block[None, :]).astype(np.float32)).astype(BF16)

    tm = _tile(t, TOKEN_TILE)
    head_blk = pl.BlockSpec((1, heads, tm, blk), lambda bi, i: (bi, 0, i, 0))
    head_arr = jax.ShapeDtypeStruct((b, heads, t, blk), BF16)
    lane_row = _resident((1, LANES))
    q, k, v = pl.pallas_call(
        functools.partial(_mla_in_kernel, heads=heads, q_rank=q_rank, kv_rank=kv_rank, nope=nope, rope=rope,
                          scale=qk_dim ** -0.5 * math.log2(math.e)),
        out_shape=[head_arr, head_arr, jax.ShapeDtypeStruct((b, heads, v_dim, t), BF16)],
        grid=(b, t // tm),
        in_specs=[pl.BlockSpec((1, tm, d), lambda bi, i: (bi, i, 0)),
                  pl.BlockSpec((1, tm, 2 * LANES), lambda bi, i: (bi, i, 0)),
                  _resident((1, d)), _resident(w_in_ext.shape), _resident((1, q_rank)), _resident((1, kv_rank)),
                  _resident(wq_ext.shape), _resident(w_kn.shape), _resident(w_vt.shape),
                  lane_row, lane_row, lane_row, lane_row, lane_row, lane_row, _resident(ones.shape)],
        out_specs=[head_blk, head_blk, pl.BlockSpec((1, heads, v_dim, tm), lambda bi, i: (bi, 0, 0, i))],
        compiler_params=_params("parallel", "parallel"),
        name="mla_in",
    )(x, cs, row(gain), w_in_ext, row(q_a_norm), row(kv_a_norm), wq_ext, w_kn, w_vt,
      row(q_norm[:nope]), row(twice(q_norm[nope:])), row(twice(rot_gain(q_norm[nope:]))),
      row(k_norm[:nope]), row(twice(k_norm[nope:])), row(twice(rot_gain(k_norm[nope:]))), ones)

    tq = _tile(t, ATTN_TILE)
    nh = ATTN_HEADS_PER_STEP
    assert heads % nh == 0
    full = pl.BlockSpec((1, nh, t, blk), lambda bi, hi, i: (bi, hi, 0, 0))
    o = pl.pallas_call(
        _flash_kernel,
        out_shape=jax.ShapeDtypeStruct((b, t, heads * v_dim), BF16),
        grid=(b, heads // nh, t // tq),
        in_specs=[pl.BlockSpec((1, nh, tq, blk), lambda bi, hi, i: (bi, hi, i, 0)), full,
                  pl.BlockSpec((1, nh, v_dim, t), lambda bi, hi, i: (bi, hi, 0, 0))],
        out_specs=pl.BlockSpec((1, tq, nh * v_dim), lambda bi, hi, i: (bi, i, hi)),
        scratch_shapes=[pltpu.VMEM((nh, tq, tq), F32), pltpu.VMEM((nh, tq, tq), F32), pltpu.VMEM((nh, 1, tq), F32),
                        pltpu.VMEM((nh, 1, tq), F32), pltpu.VMEM((nh, v_dim, tq), F32)],
        compiler_params=_params("parallel", "parallel", "arbitrary"),
        name="flash",
    )(q, k, v)

    m = b * t
    tmo = _tile(m, TOKEN_TILE)
    operands = [o.reshape(m, heads * v_dim), w_out.astype(BF16)]
    specs = [pl.BlockSpec((tmo, heads * v_dim), lambda i: (i, 0)), _resident(w_out.shape)]
    return _mla_out_mix, operands, specs


def kernel(x, positions, norm_gains, ffn_w_gate, ffn_w_up, ffn_w_down, even_w_in, even_conv_w, even_mu_shift, rwkv_w0, rwkv_w_up, rwkv_a0, rwkv_a_up, rwkv_g_up, rwkv_k_k, rwkv_k_a, rwkv_r_k, rwkv_ln_w, rwkv_ln_b, even_w_out, odd_w_in, mla_q_a_norm, mla_kv_a_norm, mla_w_q_up, mla_w_kv_up, mla_q_norm, mla_k_norm, odd_w_out):
    b, t, d = x.shape
    depth = norm_gains.shape[0]
    rope_dim = odd_w_in.shape[2] - mla_q_a_norm.shape[1] - mla_kv_a_norm.shape[1]
    cs = _rope_tables(positions, rope_dim)

    def ffn(x, layer, which, gain, mix=None):
        y = _ffn(x.reshape(b * t, d), gain, ffn_w_gate[layer, which].astype(BF16),
                 ffn_w_up[layer, which].astype(BF16), ffn_w_down[layer, which].astype(BF16), mix)
        return y.reshape(b, t, d)

    for layer in range(depth):
        g = norm_gains[layer]
        x = ffn(x, layer, 0, g[0])
        if layer % 2 == 0:
            i = layer // 2
            mix = _conv_rwkv_mixer(x, g[1], even_w_in[i], even_conv_w[i], even_mu_shift[i], rwkv_w0[i],
                                   rwkv_w_up[i], rwkv_a0[i], rwkv_a_up[i], rwkv_g_up[i], rwkv_k_k[i],
                                   rwkv_k_a[i], rwkv_r_k[i], rwkv_ln_w[i], rwkv_ln_b[i], even_w_out[i])
        else:
            j = layer // 2
            mix = _mla_mixer(x, cs, g[1], odd_w_in[j], mla_q_a_norm[j], mla_kv_a_norm[j], mla_w_q_up[j],
                             mla_w_kv_up[j], mla_q_norm[j], mla_k_norm[j], odd_w_out[j])
        x = ffn(x, layer, 1, g[2], mix)
    return x
```

```python
import functools
import math

import numpy as np
import jax
import jax.numpy as jnp
from jax import lax
from jax.experimental import pallas as pl
from jax.experimental.pallas import tpu as pltpu

F32 = jnp.float32
BF16 = jnp.bfloat16

NORM_EPS = 1e-6
RWKV_GN_EPS = 64e-5
ROPE_THETA = 10000.0
KK_NORM_FLOOR = 1e-24

LANES = 128
BF16_ROWS = 16
HALO = BF16_ROWS
VMEM_LIMIT = 56 * 1024 * 1024
RWKV_CHUNK = 64
RWKV_BLOCK = 256
TOKEN_TILE = 512
ATTN_TILE = 512
ATTN_HEADS_PER_STEP = 2


def _dot(a, b):
    return jnp.dot(a, b, preferred_element_type=F32)


def _dot_bf(a, b):
    return jnp.dot(a.astype(BF16), b.astype(BF16), preferred_element_type=F32)


def _dot_nt(a, b):
    return lax.dot_general(a, b, (((1,), (1,)), ((), ())), preferred_element_type=F32)


def _dot_tn(a, b):
    return lax.dot_general(a, b, (((0,), (0,)), ((), ())), preferred_element_type=F32)


def _split(x, parts):
    pieces = []
    for _ in range(parts):
        piece = x.astype(BF16)
        pieces.append(piece)
        x = x - piece.astype(F32)
    return pieces


def _dot_split_rhs(w, x, parts):
    return sum(_dot(w, piece) for piece in _split(x, parts))


def _dot_hi_lo(x, w_ref):
    hi, lo = _split(x, 2)
    return _dot(hi, w_ref[0]) + (_dot(hi, w_ref[1]) + _dot(lo, w_ref[0]))


def _rms(x, gain, eps=NORM_EPS):
    return x * lax.rsqrt(jnp.mean(x * x, axis=-1, keepdims=True) + eps) * gain


def _resident(shape):
    nd = len(shape)
    return pl.BlockSpec(shape, lambda *_: (0,) * nd, pipeline_mode=pl.Buffered(1))


def _params(*sem):
    return pltpu.CompilerParams(dimension_semantics=sem, vmem_limit_bytes=VMEM_LIMIT)


def _tile(n, want):
    t = min(n, want)
    assert n % t == 0, (n, t)
    return t


def _ffn_kernel(*refs, tf, n_mix, mix_fn):
    x_ref, mix_refs = refs[0], refs[1:1 + n_mix]
    gain_ref, wg_ref, wu_ref, wd_ref, o_ref, act_ref = refs[1 + n_mix:]
    x = x_ref[...]
    if mix_fn is not None:
        x = x + mix_fn(*mix_refs)
    h = _rms(x, gain_ref[...]).astype(BF16)
    for f in range(wg_ref.shape[1] // tf):
        cols = slice(f * tf, (f + 1) * tf)
        g = _dot(h, wg_ref[:, cols])
        u = _dot(h, wu_ref[:, cols])
        act_ref[:, cols] = (g * jax.nn.sigmoid(g) * u).astype(BF16)
    o_ref[...] = x + 0.5 * _dot(act_ref[...], wd_ref[...])


def _ffn(x2d, gain, wg, wu, wd, which, mix=None):
    m, d = x2d.shape
    f = wg.shape[-1]
    tm = _tile(m, TOKEN_TILE)
    tf = 256 if f % 256 == 0 else LANES
    mix_fn, mix_ops, mix_specs = mix if mix is not None else (None, [], [])
    picked = lambda rows, cols: pl.BlockSpec((None, None, rows, cols), lambda i: (*which, 0, 0),
                                             pipeline_mode=pl.Buffered(1))
    return pl.pallas_call(
        functools.partial(_ffn_kernel, tf=tf, n_mix=len(mix_ops), mix_fn=mix_fn),
        out_shape=jax.ShapeDtypeStruct((m, d), F32),
        grid=(m // tm,),
        in_specs=[pl.BlockSpec((tm, d), lambda i: (i, 0))] + list(mix_specs)
                 + [_resident((1, d)), picked(d, f), picked(d, f), picked(f, d)],
        out_specs=pl.BlockSpec((tm, d), lambda i: (i, 0)),
        scratch_shapes=[pltpu.VMEM((tm, f), BF16)],
        compiler_params=_params("parallel"),
        name="ffn",
    )(x2d, *mix_ops, gain.reshape(1, d), wg, wu, wd)


def _even_in_kernel(x_ref, xh_ref, gain_ref, w_ref, mu_ref, convw_ref, w0_ref, wup_ref, a0_ref, aup_ref,
                    gup_ref, kk_ref, ka_ref, rk_ref, hsum_ref, tri_ref,
                    a_o, r_o, b_o, k_o, v_o, bw_o, kw_o, wend_o, g_o, yc_o, bonus_o,
                    h_s, p_s, *, cw, rw, chunk):
    tm = x_ref.shape[1]
    gain = gain_ref[...]
    xh = jnp.where(pl.program_id(1) > 0, xh_ref[0], 0.0)
    h_s[0:HALO, :] = _rms(xh, gain).astype(BF16)
    h_s[HALO:, :] = _rms(x_ref[0], gain).astype(BF16)

    def proj(c0, n):
        return _dot(h_s[...], w_ref[:, c0:c0 + n])

    def staged(back, n):
        return p_s[pl.ds(HALO - back, tm), 0:n]

    gate_b = proj(0, cw)[HALO:]
    p_s[:, 0:cw] = proj(cw, cw) * proj(2 * cw, cw)
    cwt = convw_ref[...]
    yc_o[0] = (gate_b * (staged(2, cw) * cwt[0:1] + staged(1, cw) * cwt[1:2] + staged(0, cw) * cwt[2:3])).astype(BF16)

    def shifted(c0, n):
        p_s[:, 0:n] = proj(c0, n)
        cur = staged(0, n)
        return cur + (staged(1, n) - cur) * mu_ref[:, c0 - 3 * cw:c0 - 3 * cw + n]

    base = 3 * cw
    r = shifted(base, rw)
    k = shifted(base + rw, rw)
    v = shifted(base + 2 * rw, rw)
    lora = shifted(base + 3 * rw, 3 * LANES)
    dw, da, dg = lora[:, 0:LANES], lora[:, LANES:2 * LANES], lora[:, 2 * LANES:3 * LANES]

    zw = w0_ref[...] + _dot_hi_lo(jnp.tanh(dw), wup_ref)
    lw = -math.exp(-0.5) * jax.nn.sigmoid(zw)
    iclr = jax.nn.sigmoid(a0_ref[...] + _dot_bf(da, aup_ref[0]))
    g_o[0] = _dot_bf(jax.nn.sigmoid(dg), gup_ref[0]).astype(BF16)
    hsum = hsum_ref[...]
    kkr = k * kk_ref[...]
    kk = kkr * lax.rsqrt(jnp.maximum(_dot_bf(kkr * kkr, hsum), KK_NORM_FLOOR))
    kh = k * (1.0 + (iclr - 1.0) * ka_ref[...])
    bonus_o[0] = (_dot_bf(r * kh * rk_ref[...], hsum) * v).astype(BF16)
    b_vec = kk * iclr
    v_o[0] = v.astype(BF16)

    tri = tri_ref[...]
    for c in range(tm // chunk):
        rows = slice(c * chunk, (c + 1) * chunk)
        lw_c = lw[rows]
        cum = _dot_split_rhs(tri, lw_c, 2)
        tot = cum[chunk - 1:chunk, :]
        grow = jnp.exp(-cum)
        to_end = jnp.exp(tot - cum)
        a_o[0, rows, :] = (-kk[rows] * jnp.exp(cum - lw_c)).astype(BF16)
        r_o[0, rows, :] = (r[rows] * jnp.exp(cum)).astype(BF16)
        b_o[0, rows, :] = (b_vec[rows] * grow).astype(BF16)
        k_o[0, rows, :] = (kh[rows] * grow).astype(BF16)
        bw_o[0, rows, :] = (b_vec[rows] * to_end).astype(BF16)
        kw_o[0, rows, :] = (kh[rows] * to_end).astype(BF16)
        wend_o[0, c] = jnp.exp(tot)


def _rwkv_chunk_kernel(a_ref, r_ref, b_ref, k_ref, v_ref, bw_ref, kw_ref, wend_ref, y_ref, st_ref, *, chunk):
    @pl.when(pl.program_id(1) == 0)
    def _():
        st_ref[...] = jnp.zeros_like(st_ref)

    tc, width = a_ref.shape[1], a_ref.shape[2]
    c2 = 2 * chunk
    assert c2 == LANES
    n_pairs = width // LANES
    units = [(c, p) for c in range(tc // chunk) for p in range(n_pairs)]
    row = lax.broadcasted_iota(jnp.int32, (c2, c2), 0)
    col = lax.broadcasted_iota(jnp.int32, (c2, c2), 1)
    same = (row // chunk) == (col // chunk)
    strict = same & (col < row)
    incl = same & (col <= row)
    eye = row == col
    head0 = lax.broadcasted_iota(jnp.int32, (chunk, LANES), 1) < chunk

    def stacked(ref):
        out = []
        for c, p in units:
            x = ref[0, c * chunk:(c + 1) * chunk, p * LANES:(p + 1) * LANES]
            zero = jnp.zeros_like(x)
            out.append(jnp.concatenate([jnp.where(head0, x, zero), jnp.where(head0, zero, x)], axis=0))
        return out

    a_s, r_s, b_s, k_s, v_s, bw_s, kw_s = (stacked(ref) for ref in (a_ref, r_ref, b_ref, k_ref, v_ref, bw_ref, kw_ref))
    each = lambda fn: [fn(u) for u in range(len(units))]
    att = each(lambda u: _dot_nt(jnp.concatenate([a_s[u], r_s[u]], axis=0), jnp.concatenate([b_s[u], k_s[u]], axis=0)))
    power = each(lambda u: jnp.where(strict, att[u][:c2, :c2], 0.0).astype(BF16))
    a_ak = each(lambda u: jnp.where(strict, att[u][:c2, c2:], 0.0).astype(BF16))
    a_rb = each(lambda u: jnp.where(incl, att[u][c2:, :c2], 0.0).astype(BF16))
    a_rk = each(lambda u: jnp.where(incl, att[u][c2:, c2:], 0.0).astype(BF16))
    av = each(lambda u: _dot(a_ak[u], v_s[u]).astype(BF16))
    tinv = each(lambda u: jnp.where(eye, 1.0, 0.0) + power[u].astype(F32))
    for _ in range(int(math.log2(chunk)) - 1):
        power = each(lambda u: _dot(power[u], power[u]).astype(BF16))
        tinv = each(lambda u: tinv[u] + _dot(tinv[u].astype(BF16), power[u]))
    pq = each(lambda u: _dot(tinv[u].astype(BF16), jnp.concatenate([a_s[u], av[u]], axis=1)).astype(BF16))
    zero = jnp.zeros((c2, c2), BF16)
    pqv = each(lambda u: jnp.concatenate([pq[u], jnp.concatenate([zero, v_s[u]], axis=1)], axis=0))
    gy = each(lambda u: _dot(jnp.concatenate([a_rb[u], a_rk[u]], axis=1), pqv[u]))
    mn = each(lambda u: _dot_tn(jnp.concatenate([bw_s[u], kw_s[u]], axis=0), pqv[u]))
    g_mat = each(lambda u: (r_s[u].astype(F32) + gy[u][:, :c2]).astype(BF16))
    y0 = each(lambda u: gy[u][:, c2:])
    n0 = each(lambda u: mn[u][:, c2:])

    def decay_plus(u):
        c, p = units[u]
        w_end = wend_ref[0, c, :, p * LANES:(p + 1) * LANES]
        return (jnp.where(eye, w_end, 0.0) + mn[u][:, :c2]).astype(BF16)

    m_mat = each(decay_plus)
    state = [st_ref[p] for p in range(n_pairs)]
    for u, (c, p) in enumerate(units):
        s_bf = state[p].astype(BF16)
        ys = _dot(g_mat[u], s_bf) + y0[u]
        state[p] = _dot(m_mat[u], s_bf) + n0[u]
        y_ref[0, c * chunk:(c + 1) * chunk, p * LANES:(p + 1) * LANES] = (ys[:chunk] + ys[chunk:]).astype(y_ref.dtype)
    for p in range(n_pairs):
        st_ref[p] = state[p]


def _even_out_mix(y_ref, bonus_ref, g_ref, yc_ref, lnw_ref, lnb_ref, hmean_ref, wo_ref, *, cw):
    y = y_ref[...].astype(F32)
    hmean = hmean_ref[...]
    d = y - _dot_bf(y, hmean)
    var = _dot_bf(d * d, hmean)
    yn = d * lax.rsqrt(var + RWKV_GN_EPS) * lnw_ref[...] + lnb_ref[...]
    y_rwkv = (yn + bonus_ref[...].astype(F32)) * g_ref[...].astype(F32)
    return _dot(yc_ref[...], wo_ref[0:cw, :]) + _dot(y_rwkv.astype(BF16), wo_ref[cw:, :])


def _conv_rwkv_mixer(x, gain, w_in, conv_w, mu_shift, w0, w_up, a0, a_up, g_up, k_k, k_a, r_k, ln_w, ln_b, w_out):
    b, t, d = x.shape
    cw = conv_w.shape[1]
    heads, hd = r_k.shape
    rw = heads * hd
    ranks = (w_up.shape[0], a_up.shape[0], g_up.shape[0])
    assert max(ranks) <= LANES and rw % LANES == 0 and cw == rw and LANES % hd == 0

    base = 3 * cw + 3 * rw
    offs = (base, base + ranks[0], base + ranks[0] + ranks[1])

    def padded_cols(arr):
        blocks = [jnp.pad(arr[..., o:o + n], [(0, 0)] * (arr.ndim - 1) + [(0, LANES - n)])
                  for o, n in zip(offs, ranks)]
        return jnp.concatenate([arr[..., :base]] + blocks, axis=-1)

    w_cat = padded_cols(w_in).astype(BF16)
    mu_cat = padded_cols(jnp.pad(mu_shift.reshape(1, -1), ((0, 0), (3 * cw, 0))))[:, 3 * cw:]

    def pad_rows(m):
        m = jnp.pad(m, ((0, LANES - m.shape[0]), (0, 0)))
        hi = m.astype(BF16)
        return jnp.stack([hi, (m - hi.astype(F32)).astype(BF16)])

    head_id = np.arange(rw) // hd
    hsum = jnp.asarray((head_id[:, None] == head_id[None, :]).astype(np.float32)).astype(BF16)
    tri = jnp.asarray(np.tril(np.ones((RWKV_CHUNK, RWKV_CHUNK), np.float32))).astype(BF16)
    row = lambda vec: vec.reshape(1, -1)

    tm = _tile(t, TOKEN_TILE)
    wide = w_cat.shape[1]
    assert tm % RWKV_CHUNK == 0
    chunks = tm // RWKV_CHUNK
    tok = pl.BlockSpec((1, tm, rw), lambda bi, i: (bi, i, 0))
    seq_bf = jax.ShapeDtypeStruct((b, t, rw), BF16)
    a_sc, r_sc, b_sc, k_sc, v_bf, b_end, k_end, w_end, gate, y_conv, bonus = pl.pallas_call(
        functools.partial(_even_in_kernel, cw=cw, rw=rw, chunk=RWKV_CHUNK),
        out_shape=[seq_bf] * 7 + [jax.ShapeDtypeStruct((b, t // RWKV_CHUNK, 1, rw), F32)] + [seq_bf] * 3,
        grid=(b, t // tm),
        in_specs=[pl.BlockSpec((1, tm, d), lambda bi, i: (bi, i, 0)),
                  pl.BlockSpec((1, HALO, d), lambda bi, i: (bi, jnp.maximum(i * (tm // HALO) - 1, 0), 0)),
                  _resident((1, d)), _resident((d, wide)), _resident((1, wide - 3 * cw)),
                  _resident(conv_w.shape), _resident((1, rw)), _resident((2, LANES, rw)),
                  _resident((1, rw)), _resident((2, LANES, rw)), _resident((2, LANES, rw)),
                  _resident((1, rw)), _resident((1, rw)), _resident((1, rw)), _resident((rw, rw)),
                  _resident((RWKV_CHUNK, RWKV_CHUNK))],
        out_specs=[tok] * 7 + [pl.BlockSpec((1, chunks, 1, rw), lambda bi, i: (bi, i, 0, 0))] + [tok] * 3,
        scratch_shapes=[pltpu.VMEM((tm + HALO, d), BF16), pltpu.VMEM((tm + HALO, max(cw, rw)), F32)],
        compiler_params=_params("parallel", "parallel"),
        name="even_in",
    )(x, x, row(gain), w_cat, mu_cat, conv_w, row(w0), pad_rows(w_up), row(a0), pad_rows(a_up),
      pad_rows(g_up), row(k_k), row(k_a), row(r_k), hsum, tri)

    tc = _tile(t, RWKV_BLOCK)
    blk = pl.BlockSpec((1, tc, rw), lambda bi, j: (bi, j, 0))
    y = pl.pallas_call(
        functools.partial(_rwkv_chunk_kernel, chunk=RWKV_CHUNK),
        out_shape=seq_bf,
        grid=(b, t // tc),
        in_specs=[blk] * 7 + [pl.BlockSpec((1, tc // RWKV_CHUNK, 1, rw), lambda bi, j: (bi, j, 0, 0))],
        out_specs=blk,
        scratch_shapes=[pltpu.VMEM((rw // LANES, LANES, LANES), F32)],
        compiler_params=_params("parallel", "arbitrary"),
        name="rwkv_chunk",
    )(a_sc, r_sc, b_sc, k_sc, v_bf, b_end, k_end, w_end)

    m = b * t
    tmo = _tile(m, TOKEN_TILE)
    flat = lambda z: z.reshape(m, -1)
    tokf = pl.BlockSpec((tmo, rw), lambda i: (i, 0))
    operands = [flat(y), flat(bonus), flat(gate), flat(y_conv), row(ln_w), row(ln_b),
                hsum / hd,
                w_out.astype(BF16)]
    specs = [tokf, tokf, tokf, tokf, _resident((1, rw)), _resident((1, rw)), _resident((rw, rw)),
             _resident((cw + rw, d))]
    return functools.partial(_even_out_mix, cw=cw), operands, specs


def _rope_table_kernel(pos_ref, freq_ref, sel_ref, cs_ref):
    ang = pos_ref[...].astype(F32) * freq_ref[...]
    cos, sin = _split(jnp.cos(ang), 3), _split(jnp.sin(ang), 3)
    tr = ang.shape[0]
    for g in range(sel_ref.shape[0]):
        sel = sel_ref[g]
        cs_ref[g * tr:(g + 1) * tr, 0:LANES] = sum(_dot(piece, sel) for piece in cos)
        cs_ref[g * tr:(g + 1) * tr, LANES:2 * LANES] = sum(_dot(piece, sel) for piece in sin)


def _rope_tables(positions, rope_dim):
    b, t = positions.shape
    half = rope_dim // 2
    assert LANES % half == 0
    per_row = LANES // half
    tr = _tile(b * t // per_row, 256)
    tiles = b * t // (per_row * tr)
    inv_freq = ROPE_THETA ** (-jnp.arange(0, rope_dim, 2, dtype=F32) / rope_dim)
    pos_rep = jnp.repeat(positions.reshape(tiles, per_row, tr).transpose(0, 2, 1), half, axis=2).reshape(tiles * tr, LANES)
    lane = np.arange(LANES)
    sel = jnp.asarray(np.stack([(lane[:, None] == g * half + lane[None, :] % half) for g in range(per_row)])
                      .astype(np.float32)).astype(BF16)
    cs = pl.pallas_call(
        _rope_table_kernel,
        out_shape=jax.ShapeDtypeStruct((b * t, 2 * LANES), F32),
        grid=(tiles,),
        in_specs=[pl.BlockSpec((tr, LANES), lambda i: (i, 0)), _resident((1, LANES)), _resident(sel.shape)],
        out_specs=pl.BlockSpec((per_row * tr, 2 * LANES), lambda i: (i, 0)),
        compiler_params=_params("parallel"),
        name="rope_table",
    )(pos_rep, jnp.tile(inv_freq, per_row).reshape(1, LANES), sel)
    return cs.reshape(b, t, 2 * LANES)


def _mla_in_kernel(x_ref, cs_ref, gain_ref, win_ref, qa_ref, kva_ref, wq_ref, wkn_ref, wvt_ref,
                   gqn_ref, gqa_ref, gqb_ref, gkn_ref, gka_ref, gkb_ref, ones_ref, q_o, k_o, v_o,
                   *, heads, q_rank, kv_rank, nope, rope, scale):
    qk_dim = nope + rope
    tm = x_ref.shape[1]
    h = _rms(x_ref[0], gain_ref[...]).astype(BF16)
    p = _dot(h, win_ref[...])
    c_q = _rms(p[:, :q_rank], qa_ref[...]).astype(BF16)
    c_kv = _rms(p[:, q_rank:q_rank + kv_rank], kva_ref[...]).astype(BF16)
    base = q_rank + kv_rank
    k_a, k_b = p[:, base:base + LANES], p[:, base + LANES:base + 2 * LANES]
    q = _dot(c_q, wq_ref[...])
    kn = _dot(c_kv, wkn_ref[...])
    cos, sin = cs_ref[0, :, 0:LANES], cs_ref[0, :, LANES:2 * LANES]
    lane = lax.broadcasted_iota(jnp.int32, (tm, LANES), 1)
    first = lane < rope
    ones = ones_ref[...]
    pairs = heads // 2
    rope_a = [q[:, (heads + j) * LANES:(heads + j + 1) * LANES] for j in range(pairs)]
    rope_b = [q[:, (heads + pairs + j) * LANES:(heads + pairs + j + 1) * LANES] for j in range(pairs)]
    q_rot = [rope_a[j] * gqa_ref[...] * cos + rope_b[j] * gqb_ref[...] * sin for j in range(pairs)]
    k_rot = k_a * gka_ref[...] * cos + k_b * gkb_ref[...] * sin
    k_rot_sq = jnp.where(first, k_a * k_a, 0.0)
    mine = [first if hd % 2 == 0 else jnp.logical_not(first) for hd in range(heads)]
    q_n = [q[:, hd * LANES:(hd + 1) * LANES] for hd in range(heads)]
    k_n = [kn[:, hd * nope:(hd + 1) * nope] for hd in range(heads)]
    sq_q = [q_n[hd] * q_n[hd] + jnp.where(mine[hd], rope_a[hd // 2] * rope_a[hd // 2], 0.0) for hd in range(heads)]
    sq_k = [k_n[hd] * k_n[hd] + k_rot_sq for hd in range(heads)]
    pair_sum = lambda sq, j: _dot(jnp.concatenate([sq[2 * j], sq[2 * j + 1]], axis=1).astype(BF16), ones)
    ss_q = [pair_sum(sq_q, j) for j in range(pairs)]
    ss_k = [pair_sum(sq_k, j) for j in range(pairs)]
    for hd in range(heads):
        half = slice((hd % 2) * LANES, (hd % 2 + 1) * LANES)
        rs_q = lax.rsqrt(ss_q[hd // 2][:, half] / qk_dim + NORM_EPS) * scale
        rs_k = lax.rsqrt(ss_k[hd // 2][:, half] / qk_dim + NORM_EPS)
        q_o[0, hd, :, 0:nope] = (q_n[hd] * rs_q * gqn_ref[...]).astype(BF16)
        q_o[0, hd, :, nope:nope + LANES] = (jnp.where(mine[hd], q_rot[hd // 2], 0.0) * rs_q).astype(BF16)
        k_o[0, hd, :, 0:nope] = (k_n[hd] * rs_k * gkn_ref[...]).astype(BF16)
        k_o[0, hd, :, nope:nope + LANES] = (k_rot * rs_k).astype(BF16)
        v_o[0, hd] = _dot_nt(wvt_ref[hd], c_kv).astype(BF16)


def _flash_kernel(q_ref, k_ref, v_ref, o_ref, sa_ref, sb_ref, m_ref, l_ref, acc_ref):
    i = pl.program_id(2)
    nh, tq = q_ref.shape[1], q_ref.shape[2]
    vd = o_ref.shape[2] // nh
    heads = range(nh)

    def block(j):
        return pl.ds(pl.multiple_of(j * tq, tq), tq)

    def logits(h, j):
        return _dot_nt(k_ref[0, h, block(j), :], q_ref[0, h])

    def consume(h, j, s):
        m = m_ref[h]
        m_new = jnp.maximum(m, jnp.max(s, axis=0, keepdims=True))
        alpha = jnp.exp2(m - m_new)
        pr = jnp.exp2(s - m_new)
        l_ref[h] = alpha * l_ref[h] + jnp.sum(pr, axis=0, keepdims=True)
        acc_ref[h] = alpha * acc_ref[h] + _dot(v_ref[0, h, :, block(j)], pr.astype(BF16))
        m_ref[h] = m_new

    def causal(s):
        keep = lax.broadcasted_iota(jnp.int32, s.shape, 0) <= lax.broadcasted_iota(jnp.int32, s.shape, 1)
        return jnp.where(keep, s, -jnp.inf)

    m_ref[...] = jnp.full(m_ref.shape, -jnp.inf, F32)
    l_ref[...] = jnp.zeros(l_ref.shape, F32)
    acc_ref[...] = jnp.zeros(acc_ref.shape, F32)
    for h in heads:
        sa_ref[h] = logits(h, 0)

    def pair(t, carry):
        j = 2 * t
        for h in heads:
            sb_ref[h] = logits(h, j + 1)
        for h in heads:
            consume(h, j, sa_ref[h])
        for h in heads:
            sa_ref[h] = logits(h, j + 2)
        for h in heads:
            consume(h, j + 1, sb_ref[h])
        return carry

    lax.fori_loop(0, i // 2, pair, 0)

    @pl.when(i % 2 == 0)
    def _():
        for h in heads:
            consume(h, i, causal(sa_ref[h]))

    @pl.when(i % 2 == 1)
    def _():
        for h in heads:
            sb_ref[h] = logits(h, i)
        for h in heads:
            consume(h, i - 1, sa_ref[h])
        for h in heads:
            consume(h, i, causal(sb_ref[h]))

    for h in heads:
        o_ref[0, :, h * vd:(h + 1) * vd] = (acc_ref[h] / l_ref[h]).T.astype(o_ref.dtype)


def _mla_out_mix(o_ref, wo_ref):
    return _dot(o_ref[...], wo_ref[...])


def _mla_mixer(x, cs, gain, w_in, q_a_norm, kv_a_norm, w_q_up, w_kv_up, q_norm, k_norm, w_out):
    b, t, d = x.shape
    q_rank, kv_rank, qk_dim = q_a_norm.shape[0], kv_a_norm.shape[0], q_norm.shape[0]
    rope = w_in.shape[1] - q_rank - kv_rank
    nope = qk_dim - rope
    heads = w_q_up.shape[1] // qk_dim
    v_dim = w_kv_up.shape[1] // heads - nope
    half = rope // 2
    blk = nope + LANES
    assert nope == LANES and v_dim == LANES and 2 * rope == LANES and heads % 2 == 0
    assert q_rank % LANES == 0 and kv_rank % LANES == 0

    def rot_cols(w):
        return jnp.concatenate([-w[..., half:], w[..., :half]], axis=-1)

    def rot_gain(g):
        return jnp.concatenate([g[half:], g[:half]])

    twice = lambda z: jnp.concatenate([z, z], axis=-1)
    w_pe = w_in[:, q_rank + kv_rank:]
    w_in_ext = jnp.concatenate([w_in[:, :q_rank + kv_rank], twice(w_pe), twice(rot_cols(w_pe))], axis=1).astype(BF16)
    wq = w_q_up.reshape(q_rank, heads, qk_dim)
    wq_ext = jnp.concatenate([wq[..., :nope].reshape(q_rank, heads * nope),
                              wq[..., nope:].reshape(q_rank, heads * rope),
                              rot_cols(wq[..., nope:]).reshape(q_rank, heads * rope)], axis=1).astype(BF16)
    wkv = w_kv_up.reshape(kv_rank, heads, nope + v_dim)
    w_kn = wkv[..., :nope].reshape(kv_rank, heads * nope).astype(BF16)
    w_vt = wkv[..., nope:].transpose(1, 2, 0).astype(BF16)
    row = lambda vec: vec.reshape(1, -1)
    lane_block = np.arange(2 * LANES) // LANES
    ones = jnp.asarray((lane_block[:, None] == lane_block[None, :]).astype(np.float32)).astype(BF16)

    tm = _tile(t, TOKEN_TILE)
    head_blk = pl.BlockSpec((1, heads, tm, blk), lambda bi, i: (bi, 0, i, 0))
    head_arr = jax.ShapeDtypeStruct((b, heads, t, blk), BF16)
    lane_row = _resident((1, LANES))
    q, k, v = pl.pallas_call(
        functools.partial(_mla_in_kernel, heads=heads, q_rank=q_rank, kv_rank=kv_rank, nope=nope, rope=rope,
                          scale=qk_dim ** -0.5 * math.log2(math.e)),
        out_shape=[head_arr, head_arr, jax.ShapeDtypeStruct((b, heads, v_dim, t), BF16)],
        grid=(b, t // tm),
        in_specs=[pl.BlockSpec((1, tm, d), lambda bi, i: (bi, i, 0)),
                  pl.BlockSpec((1, tm, 2 * LANES), lambda bi, i: (bi, i, 0)),
                  _resident((1, d)), _resident(w_in_ext.shape), _resident((1, q_rank)), _resident((1, kv_rank)),
                  _resident(wq_ext.shape), _resident(w_kn.shape), _resident(w_vt.shape),
                  lane_row, lane_row, lane_row, lane_row, lane_row, lane_row, _resident(ones.shape)],
        out_specs=[head_blk, head_blk, pl.BlockSpec((1, heads, v_dim, tm), lambda bi, i: (bi, 0, 0, i))],
        compiler_params=_params("parallel", "parallel"),
        name="mla_in",
    )(x, cs, row(gain), w_in_ext, row(q_a_norm), row(kv_a_norm), wq_ext, w_kn, w_vt,
      row(q_norm[:nope]), row(twice(q_norm[nope:])), row(twice(rot_gain(q_norm[nope:]))),
      row(k_norm[:nope]), row(twice(k_norm[nope:])), row(twice(rot_gain(k_norm[nope:]))), ones)

    tq = _tile(t, ATTN_TILE)
    nh = ATTN_HEADS_PER_STEP
    assert heads % nh == 0
    full = pl.BlockSpec((1, nh, t, blk), lambda bi, hi, i: (bi, hi, 0, 0))
    o = pl.pallas_call(
        _flash_kernel,
        out_shape=jax.ShapeDtypeStruct((b, t, heads * v_dim), BF16),
        grid=(b, heads // nh, t // tq),
        in_specs=[pl.BlockSpec((1, nh, tq, blk), lambda bi, hi, i: (bi, hi, i, 0)), full,
                  pl.BlockSpec((1, nh, v_dim, t), lambda bi, hi, i: (bi, hi, 0, 0))],
        out_specs=pl.BlockSpec((1, tq, nh * v_dim), lambda bi, hi, i: (bi, i, hi)),
        scratch_shapes=[pltpu.VMEM((nh, tq, tq), F32), pltpu.VMEM((nh, tq, tq), F32), pltpu.VMEM((nh, 1, tq), F32),
                        pltpu.VMEM((nh, 1, tq), F32), pltpu.VMEM((nh, v_dim, tq), F32)],
        compiler_params=_params("parallel", "parallel", "arbitrary"),
        name="flash",
    )(q, k, v)

    m = b * t
    tmo = _tile(m, TOKEN_TILE)
    operands = [o.reshape(m, heads * v_dim), w_out.astype(BF16)]
    specs = [pl.BlockSpec((tmo, heads * v_dim), lambda i: (i, 0)), _resident(w_out.shape)]
    return _mla_out_mix, operands, specs


def kernel(x, positions, norm_gains, ffn_w_gate, ffn_w_up, ffn_w_down, even_w_in, even_conv_w, even_mu_shift, rwkv_w0, rwkv_w_up, rwkv_a0, rwkv_a_up, rwkv_g_up, rwkv_k_k, rwkv_k_a, rwkv_r_k, rwkv_ln_w, rwkv_ln_b, even_w_out, odd_w_in, mla_q_a_norm, mla_kv_a_norm, mla_w_q_up, mla_w_kv_up, mla_q_norm, mla_k_norm, odd_w_out):
    b, t, d = x.shape
    depth = norm_gains.shape[0]
    rope_dim = odd_w_in.shape[2] - mla_q_a_norm.shape[1] - mla_kv_a_norm.shape[1]
    cs = _rope_tables(positions, rope_dim)

    w_gate, w_up, w_down = ffn_w_gate.astype(BF16), ffn_w_up.astype(BF16), ffn_w_down.astype(BF16)

    def ffn(x, layer, which, gain, mix=None):
        return _ffn(x.reshape(b * t, d), gain, w_gate, w_up, w_down, (layer, which), mix).reshape(b, t, d)

    for layer in range(depth):
        g = norm_gains[layer]
        x = ffn(x, layer, 0, g[0])
        if layer % 2 == 0:
            i = layer // 2
            mix = _conv_rwkv_mixer(x, g[1], even_w_in[i], even_conv_w[i], even_mu_shift[i], rwkv_w0[i],
                                   rwkv_w_up[i], rwkv_a0[i], rwkv_a_up[i], rwkv_g_up[i], rwkv_k_k[i],
                                   rwkv_k_a[i], rwkv_r_k[i], rwkv_ln_w[i], rwkv_ln_b[i], even_w_out[i])
        else:
            j = layer // 2
            mix = _mla_mixer(x, cs, g[1], odd_w_in[j], mla_q_a_norm[j], mla_kv_a_norm[j], mla_w_q_up[j],
                             mla_w_kv_up[j], mla_q_norm[j], mla_k_norm[j], odd_w_out[j])
        x = ffn(x, layer, 1, g[2], mix)
    return x
```

```python
import functools
import math

import numpy as np
import jax
import jax.numpy as jnp
from jax import lax
from jax.experimental import pallas as pl
from jax.experimental.pallas import tpu as pltpu

F32 = jnp.float32
BF16 = jnp.bfloat16

NORM_EPS = 1e-6
RWKV_GN_EPS = 64e-5
ROPE_THETA = 10000.0
KK_NORM_FLOOR = 1e-24

LANES = 128
MXU_WIDTH = 256
BF16_ROWS = 16
HALO = BF16_ROWS
VMEM_LIMIT = 56 * 1024 * 1024
RWKV_CHUNK = 64
RWKV_BLOCK = 256
TOKEN_TILE = 512
ATTN_TILE = 512
ATTN_HEADS_PER_STEP = 2


def _dot(a, b):
    return jnp.dot(a, b, preferred_element_type=F32)


def _dot_bf(a, b):
    return jnp.dot(a.astype(BF16), b.astype(BF16), preferred_element_type=F32)


def _dot_nt(a, b):
    return lax.dot_general(a, b, (((1,), (1,)), ((), ())), preferred_element_type=F32)


def _dot_tn(a, b):
    return lax.dot_general(a, b, (((0,), (0,)), ((), ())), preferred_element_type=F32)


def _split(x, parts):
    pieces = []
    for _ in range(parts):
        piece = x.astype(BF16)
        pieces.append(piece)
        x = x - piece.astype(F32)
    return pieces


def _dot_split_rhs(w, x, parts):
    return sum(_dot(w, piece) for piece in _split(x, parts))


def _dot_hi_lo(x, w_ref):
    hi, lo = _split(x, 2)
    return _dot(hi, w_ref[0]) + (_dot(hi, w_ref[1]) + _dot(lo, w_ref[0]))


def _rms(x, gain, eps=NORM_EPS):
    return x * lax.rsqrt(jnp.mean(x * x, axis=-1, keepdims=True) + eps) * gain


def _resident(shape):
    nd = len(shape)
    return pl.BlockSpec(shape, lambda *_: (0,) * nd, pipeline_mode=pl.Buffered(1))


def _params(*sem):
    return pltpu.CompilerParams(dimension_semantics=sem, vmem_limit_bytes=VMEM_LIMIT)


def _tile(n, want):
    t = min(n, want)
    assert n % t == 0, (n, t)
    return t


def _ffn_kernel(*refs, tf, n_mix, mix_fn):
    x_ref, mix_refs = refs[0], refs[1:1 + n_mix]
    gain_ref, wg_ref, wu_ref, wd_ref, o_ref, act_ref = refs[1 + n_mix:]
    x = x_ref[...]
    if mix_fn is not None:
        x = x + mix_fn(*mix_refs)
    h = _rms(x, gain_ref[...]).astype(BF16)
    for f in range(wg_ref.shape[1] // tf):
        cols = slice(f * tf, (f + 1) * tf)
        g = _dot(h, wg_ref[:, cols].astype(BF16))
        u = _dot(h, wu_ref[:, cols].astype(BF16))
        act_ref[:, cols] = (g * jax.nn.sigmoid(g) * u).astype(BF16)
    o_ref[...] = x + 0.5 * _dot(act_ref[...], wd_ref[...])


def _ffn(x2d, gain, wg, wu, wd, which, mix=None):
    m, d = x2d.shape
    f = wg.shape[-1]
    tm = _tile(m, TOKEN_TILE)
    tf = MXU_WIDTH if f % MXU_WIDTH == 0 else LANES
    mix_fn, mix_ops, mix_specs = mix if mix is not None else (None, [], [])
    picked = lambda rows, cols: pl.BlockSpec((None, None, rows, cols), lambda i: (*which, 0, 0),
                                             pipeline_mode=pl.Buffered(1))
    return pl.pallas_call(
        functools.partial(_ffn_kernel, tf=tf, n_mix=len(mix_ops), mix_fn=mix_fn),
        out_shape=jax.ShapeDtypeStruct((m, d), F32),
        grid=(m // tm,),
        in_specs=[pl.BlockSpec((tm, d), lambda i: (i, 0))] + list(mix_specs)
                 + [_resident((1, d)), picked(d, f), picked(d, f), picked(f, d)],
        out_specs=pl.BlockSpec((tm, d), lambda i: (i, 0)),
        scratch_shapes=[pltpu.VMEM((tm, f), BF16)],
        compiler_params=_params("parallel"),
        name="ffn",
    )(x2d, *mix_ops, gain.reshape(1, d), wg, wu, wd)


def _even_in_kernel(x_ref, xh_ref, gain_ref, w_ref, mu_ref, convw_ref, w0_ref, wup_ref, a0_ref, aup_ref,
                    gup_ref, kk_ref, ka_ref, rk_ref, hsum_ref, tri_ref,
                    a_o, r_o, b_o, k_o, v_o, bw_o, kw_o, wend_o, g_o, yc_o, bonus_o,
                    h_s, p_s, *, cw, rw, chunk):
    tm = x_ref.shape[1]
    gain = gain_ref[...]
    xh = jnp.where(pl.program_id(1) > 0, xh_ref[0], 0.0)
    h_s[0:HALO, :] = _rms(xh, gain).astype(BF16)
    h_s[HALO:, :] = _rms(x_ref[0], gain).astype(BF16)

    def proj(c0, n):
        return _dot(h_s[...], w_ref[:, c0:c0 + n])

    def staged(back, n):
        return p_s[pl.ds(HALO - back, tm), 0:n]

    gate_b = proj(0, cw)[HALO:]
    p_s[:, 0:cw] = proj(cw, cw) * proj(2 * cw, cw)
    cwt = convw_ref[...]
    yc_o[0] = (gate_b * (staged(2, cw) * cwt[0:1] + staged(1, cw) * cwt[1:2] + staged(0, cw) * cwt[2:3])).astype(BF16)

    def shifted(c0, n):
        p_s[:, 0:n] = proj(c0, n)
        cur = staged(0, n)
        return cur + (staged(1, n) - cur) * mu_ref[:, c0 - 3 * cw:c0 - 3 * cw + n]

    base = 3 * cw
    r = shifted(base, rw)
    k = shifted(base + rw, rw)
    v = shifted(base + 2 * rw, rw)
    lora = shifted(base + 3 * rw, 3 * LANES)
    dw, da, dg = lora[:, 0:LANES], lora[:, LANES:2 * LANES], lora[:, 2 * LANES:3 * LANES]

    zw = w0_ref[...] + _dot_hi_lo(jnp.tanh(dw), wup_ref)
    lw = -math.exp(-0.5) * jax.nn.sigmoid(zw)
    iclr = jax.nn.sigmoid(a0_ref[...] + _dot_bf(da, aup_ref[0]))
    g_o[0] = _dot_bf(jax.nn.sigmoid(dg), gup_ref[0]).astype(BF16)
    hsum = hsum_ref[...]
    kkr = k * kk_ref[...]
    kk = kkr * lax.rsqrt(jnp.maximum(_dot_bf(kkr * kkr, hsum), KK_NORM_FLOOR))
    kh = k * (1.0 + (iclr - 1.0) * ka_ref[...])
    bonus_o[0] = (_dot_bf(r * kh * rk_ref[...], hsum) * v).astype(BF16)
    b_vec = kk * iclr
    v_o[0] = v.astype(BF16)

    tri = tri_ref[...]
    for c in range(tm // chunk):
        rows = slice(c * chunk, (c + 1) * chunk)
        lw_c = lw[rows]
        cum = _dot_split_rhs(tri, lw_c, 2)
        tot = cum[chunk - 1:chunk, :]
        grow = jnp.exp(-cum)
        to_end = jnp.exp(tot - cum)
        a_o[0, rows, :] = (-kk[rows] * jnp.exp(cum - lw_c)).astype(BF16)
        r_o[0, rows, :] = (r[rows] * jnp.exp(cum)).astype(BF16)
        b_o[0, rows, :] = (b_vec[rows] * grow).astype(BF16)
        k_o[0, rows, :] = (kh[rows] * grow).astype(BF16)
        bw_o[0, rows, :] = (b_vec[rows] * to_end).astype(BF16)
        kw_o[0, rows, :] = (kh[rows] * to_end).astype(BF16)
        wend_o[0, c] = jnp.exp(tot)


def _rwkv_chunk_kernel(a_ref, r_ref, b_ref, k_ref, v_ref, bw_ref, kw_ref, wend_ref, y_ref, st_ref, *, chunk):
    @pl.when(pl.program_id(1) == 0)
    def _():
        st_ref[...] = jnp.zeros_like(st_ref)

    tc, width = a_ref.shape[1], a_ref.shape[2]
    c2 = 2 * chunk
    assert c2 == LANES
    n_pairs = width // LANES
    units = [(c, p) for c in range(tc // chunk) for p in range(n_pairs)]
    row = lax.broadcasted_iota(jnp.int32, (c2, c2), 0)
    col = lax.broadcasted_iota(jnp.int32, (c2, c2), 1)
    same = (row // chunk) == (col // chunk)
    strict = same & (col < row)
    incl = same & (col <= row)
    eye = row == col
    head0 = lax.broadcasted_iota(jnp.int32, (chunk, LANES), 1) < chunk

    def stacked(ref):
        out = []
        for c, p in units:
            x = ref[0, c * chunk:(c + 1) * chunk, p * LANES:(p + 1) * LANES]
            zero = jnp.zeros_like(x)
            out.append(jnp.concatenate([jnp.where(head0, x, zero), jnp.where(head0, zero, x)], axis=0))
        return out

    a_s, r_s, b_s, k_s, v_s, bw_s, kw_s = (stacked(ref) for ref in (a_ref, r_ref, b_ref, k_ref, v_ref, bw_ref, kw_ref))
    each = lambda fn: [fn(u) for u in range(len(units))]
    att = each(lambda u: _dot_nt(jnp.concatenate([a_s[u], r_s[u]], axis=0), jnp.concatenate([b_s[u], k_s[u]], axis=0)))
    power = each(lambda u: jnp.where(strict, att[u][:c2, :c2], 0.0).astype(BF16))
    a_ak = each(lambda u: jnp.where(strict, att[u][:c2, c2:], 0.0).astype(BF16))
    a_rb = each(lambda u: jnp.where(incl, att[u][c2:, :c2], 0.0).astype(BF16))
    a_rk = each(lambda u: jnp.where(incl, att[u][c2:, c2:], 0.0).astype(BF16))
    av = each(lambda u: _dot(a_ak[u], v_s[u]).astype(BF16))
    tinv = each(lambda u: jnp.where(eye, 1.0, 0.0) + power[u].astype(F32))
    for _ in range(int(math.log2(chunk)) - 1):
        power = each(lambda u: _dot(power[u], power[u]).astype(BF16))
        tinv = each(lambda u: tinv[u] + _dot(tinv[u].astype(BF16), power[u]))
    pq = each(lambda u: _dot(tinv[u].astype(BF16), jnp.concatenate([a_s[u], av[u]], axis=1)).astype(BF16))
    zero = jnp.zeros((c2, c2), BF16)
    pqv = each(lambda u: jnp.concatenate([pq[u], jnp.concatenate([zero, v_s[u]], axis=1)], axis=0))
    gy = each(lambda u: _dot(jnp.concatenate([a_rb[u], a_rk[u]], axis=1), pqv[u]))
    mn = each(lambda u: _dot_tn(jnp.concatenate([bw_s[u], kw_s[u]], axis=0), pqv[u]))
    g_mat = each(lambda u: (r_s[u].astype(F32) + gy[u][:, :c2]).astype(BF16))
    y0 = each(lambda u: gy[u][:, c2:])
    n0 = each(lambda u: mn[u][:, c2:])

    def decay_plus(u):
        c, p = units[u]
        w_end = wend_ref[0, c, :, p * LANES:(p + 1) * LANES]
        return (jnp.where(eye, w_end, 0.0) + mn[u][:, :c2]).astype(BF16)

    m_mat = each(decay_plus)
    state = [st_ref[p] for p in range(n_pairs)]
    for u, (c, p) in enumerate(units):
        s_bf = state[p].astype(BF16)
        ys = _dot(g_mat[u], s_bf) + y0[u]
        state[p] = _dot(m_mat[u], s_bf) + n0[u]
        y_ref[0, c * chunk:(c + 1) * chunk, p * LANES:(p + 1) * LANES] = (ys[:chunk] + ys[chunk:]).astype(y_ref.dtype)
    for p in range(n_pairs):
        st_ref[p] = state[p]


def _even_out_mix(y_ref, bonus_ref, g_ref, yc_ref, lnw_ref, lnb_ref, hmean_ref, wo_ref, *, cw):
    y = y_ref[...].astype(F32)
    hmean = hmean_ref[...]
    d = y - _dot_bf(y, hmean)
    var = _dot_bf(d * d, hmean)
    yn = d * lax.rsqrt(var + RWKV_GN_EPS) * lnw_ref[...] + lnb_ref[...]
    y_rwkv = (yn + bonus_ref[...].astype(F32)) * g_ref[...].astype(F32)
    return _dot(yc_ref[...], wo_ref[0:cw, :]) + _dot(y_rwkv.astype(BF16), wo_ref[cw:, :])


def _conv_rwkv_mixer(x, gain, w_in, conv_w, mu_shift, w0, w_up, a0, a_up, g_up, k_k, k_a, r_k, ln_w, ln_b, w_out):
    b, t, d = x.shape
    cw = conv_w.shape[1]
    heads, hd = r_k.shape
    rw = heads * hd
    ranks = (w_up.shape[0], a_up.shape[0], g_up.shape[0])
    assert max(ranks) <= LANES and rw % LANES == 0 and cw == rw and LANES % hd == 0

    base = 3 * cw + 3 * rw
    offs = (base, base + ranks[0], base + ranks[0] + ranks[1])

    def padded_cols(arr):
        blocks = [jnp.pad(arr[..., o:o + n], [(0, 0)] * (arr.ndim - 1) + [(0, LANES - n)])
                  for o, n in zip(offs, ranks)]
        return jnp.concatenate([arr[..., :base]] + blocks, axis=-1)

    w_cat = padded_cols(w_in).astype(BF16)
    mu_cat = padded_cols(jnp.pad(mu_shift.reshape(1, -1), ((0, 0), (3 * cw, 0))))[:, 3 * cw:]

    def pad_rows(m):
        m = jnp.pad(m, ((0, LANES - m.shape[0]), (0, 0)))
        hi = m.astype(BF16)
        return jnp.stack([hi, (m - hi.astype(F32)).astype(BF16)])

    head_id = np.arange(rw) // hd
    hsum = jnp.asarray((head_id[:, None] == head_id[None, :]).astype(np.float32)).astype(BF16)
    tri = jnp.asarray(np.tril(np.ones((RWKV_CHUNK, RWKV_CHUNK), np.float32))).astype(BF16)
    row = lambda vec: vec.reshape(1, -1)

    tm = _tile(t, TOKEN_TILE)
    wide = w_cat.shape[1]
    assert tm % RWKV_CHUNK == 0
    chunks = tm // RWKV_CHUNK
    tok = pl.BlockSpec((1, tm, rw), lambda bi, i: (bi, i, 0))
    seq_bf = jax.ShapeDtypeStruct((b, t, rw), BF16)
    a_sc, r_sc, b_sc, k_sc, v_bf, b_end, k_end, w_end, gate, y_conv, bonus = pl.pallas_call(
        functools.partial(_even_in_kernel, cw=cw, rw=rw, chunk=RWKV_CHUNK),
        out_shape=[seq_bf] * 7 + [jax.ShapeDtypeStruct((b, t // RWKV_CHUNK, 1, rw), F32)] + [seq_bf] * 3,
        grid=(b, t // tm),
        in_specs=[pl.BlockSpec((1, tm, d), lambda bi, i: (bi, i, 0)),
                  pl.BlockSpec((1, HALO, d), lambda bi, i: (bi, jnp.maximum(i * (tm // HALO) - 1, 0), 0)),
                  _resident((1, d)), _resident((d, wide)), _resident((1, wide - 3 * cw)),
                  _resident(conv_w.shape), _resident((1, rw)), _resident((2, LANES, rw)),
                  _resident((1, rw)), _resident((2, LANES, rw)), _resident((2, LANES, rw)),
                  _resident((1, rw)), _resident((1, rw)), _resident((1, rw)), _resident((rw, rw)),
                  _resident((RWKV_CHUNK, RWKV_CHUNK))],
        out_specs=[tok] * 7 + [pl.BlockSpec((1, chunks, 1, rw), lambda bi, i: (bi, i, 0, 0))] + [tok] * 3,
        scratch_shapes=[pltpu.VMEM((tm + HALO, d), BF16), pltpu.VMEM((tm + HALO, max(cw, rw)), F32)],
        compiler_params=_params("parallel", "parallel"),
        name="even_in",
    )(x, x, row(gain), w_cat, mu_cat, conv_w, row(w0), pad_rows(w_up), row(a0), pad_rows(a_up),
      pad_rows(g_up), row(k_k), row(k_a), row(r_k), hsum, tri)

    tc = _tile(t, RWKV_BLOCK)
    blk = pl.BlockSpec((1, tc, rw), lambda bi, j: (bi, j, 0))
    y = pl.pallas_call(
        functools.partial(_rwkv_chunk_kernel, chunk=RWKV_CHUNK),
        out_shape=seq_bf,
        grid=(b, t // tc),
        in_specs=[blk] * 7 + [pl.BlockSpec((1, tc // RWKV_CHUNK, 1, rw), lambda bi, j: (bi, j, 0, 0))],
        out_specs=blk,
        scratch_shapes=[pltpu.VMEM((rw // LANES, LANES, LANES), F32)],
        compiler_params=_params("parallel", "arbitrary"),
        name="rwkv_chunk",
    )(a_sc, r_sc, b_sc, k_sc, v_bf, b_end, k_end, w_end)

    m = b * t
    tmo = _tile(m, TOKEN_TILE)
    flat = lambda z: z.reshape(m, -1)
    tokf = pl.BlockSpec((tmo, rw), lambda i: (i, 0))
    operands = [flat(y), flat(bonus), flat(gate), flat(y_conv), row(ln_w), row(ln_b),
                hsum / hd,
                w_out.astype(BF16)]
    specs = [tokf, tokf, tokf, tokf, _resident((1, rw)), _resident((1, rw)), _resident((rw, rw)),
             _resident((cw + rw, d))]
    return functools.partial(_even_out_mix, cw=cw), operands, specs


def _rope_table_kernel(pos_ref, freq_ref, sel_ref, cs_ref):
    ang = pos_ref[...].astype(F32) * freq_ref[...]
    cos, sin = _split(jnp.cos(ang), 3), _split(jnp.sin(ang), 3)
    tr = ang.shape[0]
    for g in range(sel_ref.shape[0]):
        sel = sel_ref[g]
        cs_ref[g * tr:(g + 1) * tr, 0:LANES] = sum(_dot(piece, sel) for piece in cos)
        cs_ref[g * tr:(g + 1) * tr, LANES:2 * LANES] = sum(_dot(piece, sel) for piece in sin)


def _rope_tables(positions, rope_dim):
    b, t = positions.shape
    half = rope_dim // 2
    assert LANES % half == 0
    per_row = LANES // half
    tr = _tile(b * t // per_row, 256)
    tiles = b * t // (per_row * tr)
    inv_freq = ROPE_THETA ** (-jnp.arange(0, rope_dim, 2, dtype=F32) / rope_dim)
    pos_rep = jnp.repeat(positions.reshape(tiles, per_row, tr).transpose(0, 2, 1), half, axis=2).reshape(tiles * tr, LANES)
    lane = np.arange(LANES)
    sel = jnp.asarray(np.stack([(lane[:, None] == g * half + lane[None, :] % half) for g in range(per_row)])
                      .astype(np.float32)).astype(BF16)
    cs = pl.pallas_call(
        _rope_table_kernel,
        out_shape=jax.ShapeDtypeStruct((b * t, 2 * LANES), F32),
        grid=(tiles,),
        in_specs=[pl.BlockSpec((tr, LANES), lambda i: (i, 0)), _resident((1, LANES)), _resident(sel.shape)],
        out_specs=pl.BlockSpec((per_row * tr, 2 * LANES), lambda i: (i, 0)),
        compiler_params=_params("parallel"),
        name="rope_table",
    )(pos_rep, jnp.tile(inv_freq, per_row).reshape(1, LANES), sel)
    return cs.reshape(b, t, 2 * LANES)


def _mla_in_kernel(x_ref, cs_ref, gain_ref, win_ref, qa_ref, kva_ref, wq_ref, wkn_ref, wvt_ref,
                   gqn_ref, gqa_ref, gqb_ref, gkn_ref, gka_ref, gkb_ref, ones_ref, q_o, k_o, v_o,
                   *, heads, q_rank, kv_rank, nope, rope, scale):
    qk_dim = nope + rope
    tm = x_ref.shape[1]
    h = _rms(x_ref[0], gain_ref[...]).astype(BF16)
    p = _dot(h, win_ref[...])
    c_q = _rms(p[:, :q_rank], qa_ref[...]).astype(BF16)
    c_kv = _rms(p[:, q_rank:q_rank + kv_rank], kva_ref[...]).astype(BF16)
    base = q_rank + kv_rank
    k_a, k_b = p[:, base:base + LANES], p[:, base + LANES:base + 2 * LANES]
    q = _dot(c_q, wq_ref[...])
    kn = _dot(c_kv, wkn_ref[...])
    cos, sin = cs_ref[0, :, 0:LANES], cs_ref[0, :, LANES:2 * LANES]
    lane = lax.broadcasted_iota(jnp.int32, (tm, LANES), 1)
    first = lane < rope
    ones = ones_ref[...]
    pairs = heads // 2
    rope_a = [q[:, (heads + j) * LANES:(heads + j + 1) * LANES] for j in range(pairs)]
    rope_b = [q[:, (heads + pairs + j) * LANES:(heads + pairs + j + 1) * LANES] for j in range(pairs)]
    q_rot = [rope_a[j] * gqa_ref[...] * cos + rope_b[j] * gqb_ref[...] * sin for j in range(pairs)]
    k_rot = k_a * gka_ref[...] * cos + k_b * gkb_ref[...] * sin
    k_rot_sq = jnp.where(first, k_a * k_a, 0.0)
    mine = [first if hd % 2 == 0 else jnp.logical_not(first) for hd in range(heads)]
    q_n = [q[:, hd * LANES:(hd + 1) * LANES] for hd in range(heads)]
    k_n = [kn[:, hd * nope:(hd + 1) * nope] for hd in range(heads)]
    sq_q = [q_n[hd] * q_n[hd] + jnp.where(mine[hd], rope_a[hd // 2] * rope_a[hd // 2], 0.0) for hd in range(heads)]
    sq_k = [k_n[hd] * k_n[hd] + k_rot_sq for hd in range(heads)]
    pair_sum = lambda sq, j: _dot(jnp.concatenate([sq[2 * j], sq[2 * j + 1]], axis=1).astype(BF16), ones)
    ss_q = [pair_sum(sq_q, j) for j in range(pairs)]
    ss_k = [pair_sum(sq_k, j) for j in range(pairs)]
    for hd in range(heads):
        half = slice((hd % 2) * LANES, (hd % 2 + 1) * LANES)
        rs_q = lax.rsqrt(ss_q[hd // 2][:, half] / qk_dim + NORM_EPS) * scale
        rs_k = lax.rsqrt(ss_k[hd // 2][:, half] / qk_dim + NORM_EPS)
        q_o[0, hd, :, 0:nope] = (q_n[hd] * rs_q * gqn_ref[...]).astype(BF16)
        q_o[0, hd, :, nope:nope + LANES] = (jnp.where(mine[hd], q_rot[hd // 2], 0.0) * rs_q).astype(BF16)
        k_o[0, hd, :, 0:nope] = (k_n[hd] * rs_k * gkn_ref[...]).astype(BF16)
        k_o[0, hd, :, nope:nope + LANES] = (k_rot * rs_k).astype(BF16)
        v_o[0, hd] = _dot_nt(wvt_ref[hd], c_kv).astype(BF16)


def _flash_kernel(q_ref, k_ref, v_ref, o_ref, sa_ref, sb_ref, m_ref, l_ref, acc_ref):
    i = pl.program_id(2)
    nh, tq = q_ref.shape[1], q_ref.shape[2]
    vd = o_ref.shape[2] // nh
    heads = range(nh)

    def block(j):
        return pl.ds(pl.multiple_of(j * tq, tq), tq)

    def logits(h, j):
        return _dot_nt(k_ref[0, h, block(j), :], q_ref[0, h])

    def consume(h, j, s):
        m = m_ref[h]
        m_new = jnp.maximum(m, jnp.max(s, axis=0, keepdims=True))
        alpha = jnp.exp2(m - m_new)
        pr = jnp.exp2(s - m_new)
        l_ref[h] = alpha * l_ref[h] + jnp.sum(pr, axis=0, keepdims=True)
        acc_ref[h] = alpha * acc_ref[h] + _dot(v_ref[0, h, :, block(j)], pr.astype(BF16))
        m_ref[h] = m_new

    def causal(s):
        keep = lax.broadcasted_iota(jnp.int32, s.shape, 0) <= lax.broadcasted_iota(jnp.int32, s.shape, 1)
        return jnp.where(keep, s, -jnp.inf)

    m_ref[...] = jnp.full(m_ref.shape, -jnp.inf, F32)
    l_ref[...] = jnp.zeros(l_ref.shape, F32)
    acc_ref[...] = jnp.zeros(acc_ref.shape, F32)
    for h in heads:
        sa_ref[h] = logits(h, 0)

    def pair(t, carry):
        j = 2 * t
        for h in heads:
            sb_ref[h] = logits(h, j + 1)
        for h in heads:
            consume(h, j, sa_ref[h])
        for h in heads:
            sa_ref[h] = logits(h, j + 2)
        for h in heads:
            consume(h, j + 1, sb_ref[h])
        return carry

    lax.fori_loop(0, i // 2, pair, 0)

    @pl.when(i % 2 == 0)
    def _():
        for h in heads:
            consume(h, i, causal(sa_ref[h]))

    @pl.when(i % 2 == 1)
    def _():
        for h in heads:
            sb_ref[h] = logits(h, i)
        for h in heads:
            consume(h, i - 1, sa_ref[h])
        for h in heads:
            consume(h, i, causal(sb_ref[h]))

    for h in heads:
        o_ref[0, :, h * vd:(h + 1) * vd] = (acc_ref[h] / l_ref[h]).T.astype(o_ref.dtype)


def _mla_out_mix(o_ref, wo_ref):
    return _dot(o_ref[...], wo_ref[...])


def _mla_mixer(x, cs, gain, w_in, q_a_norm, kv_a_norm, w_q_up, w_kv_up, q_norm, k_norm, w_out):
    b, t, d = x.shape
    q_rank, kv_rank, qk_dim = q_a_norm.shape[0], kv_a_norm.shape[0], q_norm.shape[0]
    rope = w_in.shape[1] - q_rank - kv_rank
    nope = qk_dim - rope
    heads = w_q_up.shape[1] // qk_dim
    v_dim = w_kv_up.shape[1] // heads - nope
    half = rope // 2
    blk = nope + LANES
    assert nope == LANES and v_dim == LANES and 2 * rope == LANES and heads % 2 == 0
    assert q_rank % LANES == 0 and kv_rank % LANES == 0

    def rot_cols(w):
        return jnp.concatenate([-w[..., half:], w[..., :half]], axis=-1)

    def rot_gain(g):
        return jnp.concatenate([g[half:], g[:half]])

    twice = lambda z: jnp.concatenate([z, z], axis=-1)
    w_pe = w_in[:, q_rank + kv_rank:]
    w_in_ext = jnp.concatenate([w_in[:, :q_rank + kv_rank], twice(w_pe), twice(rot_cols(w_pe))], axis=1).astype(BF16)
    wq = w_q_up.reshape(q_rank, heads, qk_dim)
    wq_ext = jnp.concatenate([wq[..., :nope].reshape(q_rank, heads * nope),
                              wq[..., nope:].reshape(q_rank, heads * rope),
                              rot_cols(wq[..., nope:]).reshape(q_rank, heads * rope)], axis=1).astype(BF16)
    wkv = w_kv_up.reshape(kv_rank, heads, nope + v_dim)
    w_kn = wkv[..., :nope].reshape(kv_rank, heads * nope).astype(BF16)
    w_vt = wkv[..., nope:].transpose(1, 2, 0).astype(BF16)
    row = lambda vec: vec.reshape(1, -1)
    lane_block = np.arange(2 * LANES) // LANES
    ones = jnp.asarray((lane_block[:, None] == lane_block[None, :]).astype(np.float32)).astype(BF16)

    tm = _tile(t, TOKEN_TILE)
    head_blk = pl.BlockSpec((1, heads, tm, blk), lambda bi, i: (bi, 0, i, 0))
    head_arr = jax.ShapeDtypeStruct((b, heads, t, blk), BF16)
    lane_row = _resident((1, LANES))
    q, k, v = pl.pallas_call(
        functools.partial(_mla_in_kernel, heads=heads, q_rank=q_rank, kv_rank=kv_rank, nope=nope, rope=rope,
                          scale=qk_dim ** -0.5 * math.log2(math.e)),
        out_shape=[head_arr, head_arr, jax.ShapeDtypeStruct((b, heads, v_dim, t), BF16)],
        grid=(b, t // tm),
        in_specs=[pl.BlockSpec((1, tm, d), lambda bi, i: (bi, i, 0)),
                  pl.BlockSpec((1, tm, 2 * LANES), lambda bi, i: (bi, i, 0)),
                  _resident((1, d)), _resident(w_in_ext.shape), _resident((1, q_rank)), _resident((1, kv_rank)),
                  _resident(wq_ext.shape), _resident(w_kn.shape), _resident(w_vt.shape),
                  lane_row, lane_row, lane_row, lane_row, lane_row, lane_row, _resident(ones.shape)],
        out_specs=[head_blk, head_blk, pl.BlockSpec((1, heads, v_dim, tm), lambda bi, i: (bi, 0, 0, i))],
        compiler_params=_params("parallel", "parallel"),
        name="mla_in",
    )(x, cs, row(gain), w_in_ext, row(q_a_norm), row(kv_a_norm), wq_ext, w_kn, w_vt,
      row(q_norm[:nope]), row(twice(q_norm[nope:])), row(twice(rot_gain(q_norm[nope:]))),
      row(k_norm[:nope]), row(twice(k_norm[nope:])), row(twice(rot_gain(k_norm[nope:]))), ones)

    tq = _tile(t, ATTN_TILE)
    nh = ATTN_HEADS_PER_STEP
    assert heads % nh == 0
    full = pl.BlockSpec((1, nh, t, blk), lambda bi, hi, i: (bi, hi, 0, 0))
    o = pl.pallas_call(
        _flash_kernel,
        out_shape=jax.ShapeDtypeStruct((b, t, heads * v_dim), BF16),
        grid=(b, heads // nh, t // tq),
        in_specs=[pl.BlockSpec((1, nh, tq, blk), lambda bi, hi, i: (bi, hi, i, 0)), full,
                  pl.BlockSpec((1, nh, v_dim, t), lambda bi, hi, i: (bi, hi, 0, 0))],
        out_specs=pl.BlockSpec((1, tq, nh * v_dim), lambda bi, hi, i: (bi, i, hi)),
        scratch_shapes=[pltpu.VMEM((nh, tq, tq), F32), pltpu.VMEM((nh, tq, tq), F32), pltpu.VMEM((nh, 1, tq), F32),
                        pltpu.VMEM((nh, 1, tq), F32), pltpu.VMEM((nh, v_dim, tq), F32)],
        compiler_params=_params("parallel", "parallel", "arbitrary"),
        name="flash",
    )(q, k, v)

    m = b * t
    tmo = _tile(m, TOKEN_TILE)
    operands = [o.reshape(m, heads * v_dim), w_out.astype(BF16)]
    specs = [pl.BlockSpec((tmo, heads * v_dim), lambda i: (i, 0)), _resident(w_out.shape)]
    return _mla_out_mix, operands, specs


def kernel(x, positions, norm_gains, ffn_w_gate, ffn_w_up, ffn_w_down, even_w_in, even_conv_w, even_mu_shift, rwkv_w0, rwkv_w_up, rwkv_a0, rwkv_a_up, rwkv_g_up, rwkv_k_k, rwkv_k_a, rwkv_r_k, rwkv_ln_w, rwkv_ln_b, even_w_out, odd_w_in, mla_q_a_norm, mla_kv_a_norm, mla_w_q_up, mla_w_kv_up, mla_q_norm, mla_k_norm, odd_w_out):
    b, t, d = x.shape
    depth = norm_gains.shape[0]
    rope_dim = odd_w_in.shape[2] - mla_q_a_norm.shape[1] - mla_kv_a_norm.shape[1]
    cs = _rope_tables(positions, rope_dim)

    w_gate, w_up, w_down = ffn_w_gate, ffn_w_up, ffn_w_down.astype(BF16)

    def ffn(x, layer, which, gain, mix=None):
        return _ffn(x.reshape(b * t, d), gain, w_gate, w_up, w_down, (layer, which), mix).reshape(b, t, d)

    for layer in range(depth):
        g = norm_gains[layer]
        x = ffn(x, layer, 0, g[0])
        if layer % 2 == 0:
            i = layer // 2
            mix = _conv_rwkv_mixer(x, g[1], even_w_in[i], even_conv_w[i], even_mu_shift[i], rwkv_w0[i],
                                   rwkv_w_up[i], rwkv_a0[i], rwkv_a_up[i], rwkv_g_up[i], rwkv_k_k[i],
                                   rwkv_k_a[i], rwkv_r_k[i], rwkv_ln_w[i], rwkv_ln_b[i], even_w_out[i])
        else:
            j = layer // 2
            mix = _mla_mixer(x, cs, g[1], odd_w_in[j], mla_q_a_norm[j], mla_kv_a_norm[j], mla_w_q_up[j],
                             mla_w_kv_up[j], mla_q_norm[j], mla_k_norm[j], odd_w_out[j])
        x = ffn(x, layer, 1, g[2], mix)
    return x
```

```python
import functools
import math

import numpy as np
import jax
import jax.numpy as jnp
from jax import lax
from jax.experimental import pallas as pl
from jax.experimental.pallas import tpu as pltpu

F32 = jnp.float32
BF16 = jnp.bfloat16

NORM_EPS = 1e-6
RWKV_GN_EPS = 64e-5
ROPE_THETA = 10000.0
KK_NORM_FLOOR = 1e-24

LANES = 128
BF16_ROWS = 16
HALO = BF16_ROWS
VMEM_LIMIT = 56 * 1024 * 1024
RWKV_CHUNK = 64
RWKV_BLOCK = 256
TOKEN_TILE = 512
ATTN_TILE = 512
ATTN_HEADS_PER_STEP = 2


def _dot(a, b):
    return jnp.dot(a, b, preferred_element_type=F32)


def _dot_bf(a, b):
    return jnp.dot(a.astype(BF16), b.astype(BF16), preferred_element_type=F32)


def _dot_nt(a, b):
    return lax.dot_general(a, b, (((1,), (1,)), ((), ())), preferred_element_type=F32)


def _dot_tn(a, b):
    return lax.dot_general(a, b, (((0,), (0,)), ((), ())), preferred_element_type=F32)


def _split(x, parts):
    pieces = []
    for _ in range(parts):
        piece = x.astype(BF16)
        pieces.append(piece)
        x = x - piece.astype(F32)
    return pieces


def _dot_split_rhs(w, x, parts):
    return sum(_dot(w, piece) for piece in _split(x, parts))


def _dot_hi_lo(x, w_ref):
    hi, lo = _split(x, 2)
    return _dot(hi, w_ref[0]) + (_dot(hi, w_ref[1]) + _dot(lo, w_ref[0]))


def _rms(x, gain, eps=NORM_EPS):
    return x * lax.rsqrt(jnp.mean(x * x, axis=-1, keepdims=True) + eps) * gain


def _resident(shape):
    nd = len(shape)
    return pl.BlockSpec(shape, lambda *_: (0,) * nd, pipeline_mode=pl.Buffered(1))


def _params(*sem):
    return pltpu.CompilerParams(dimension_semantics=sem, vmem_limit_bytes=VMEM_LIMIT)


def _tile(n, want):
    t = min(n, want)
    assert n % t == 0, (n, t)
    return t


def _ffn_kernel(*refs, tf, n_mix, mix_fn):
    x_ref, mix_refs = refs[0], refs[1:1 + n_mix]
    gain_ref, wg_ref, wu_ref, wd_ref, o_ref, act_ref = refs[1 + n_mix:]
    x = x_ref[...]
    if mix_fn is not None:
        x = x + mix_fn(*mix_refs)
    h = _rms(x, gain_ref[...]).astype(BF16)
    for f in range(wg_ref.shape[1] // tf):
        cols = slice(f * tf, (f + 1) * tf)
        g = _dot(h, wg_ref[:, cols])
        u = _dot(h, wu_ref[:, cols])
        act_ref[:, cols] = (g * jax.nn.sigmoid(g) * u).astype(BF16)
    o_ref[...] = x + 0.5 * _dot(act_ref[...], wd_ref[...])


def _ffn(x2d, gain, wg, wu, wd, which, mix=None):
    m, d = x2d.shape
    f = wg.shape[-1]
    tm = _tile(m, TOKEN_TILE)
    tf = 256 if f % 256 == 0 else LANES
    mix_fn, mix_ops, mix_specs = mix if mix is not None else (None, [], [])
    picked = lambda rows, cols: pl.BlockSpec((None, None, rows, cols), lambda i: (*which, 0, 0),
                                             pipeline_mode=pl.Buffered(1))
    return pl.pallas_call(
        functools.partial(_ffn_kernel, tf=tf, n_mix=len(mix_ops), mix_fn=mix_fn),
        out_shape=jax.ShapeDtypeStruct((m, d), F32),
        grid=(m // tm,),
        in_specs=[pl.BlockSpec((tm, d), lambda i: (i, 0))] + list(mix_specs)
                 + [_resident((1, d)), picked(d, f), picked(d, f), picked(f, d)],
        out_specs=pl.BlockSpec((tm, d), lambda i: (i, 0)),
        scratch_shapes=[pltpu.VMEM((tm, f), BF16)],
        compiler_params=_params("parallel"),
        name="ffn",
    )(x2d, *mix_ops, gain.reshape(1, d), wg, wu, wd)


def _even_in_kernel(x_ref, xh_ref, gain_ref, w_ref, mu_ref, convw_ref, w0_ref, wup_ref, a0_ref, aup_ref,
                    gup_ref, kk_ref, ka_ref, rk_ref, hsum_ref, tri_ref,
                    a_o, r_o, b_o, k_o, v_o, bw_o, kw_o, wend_o, g_o, yc_o, bonus_o,
                    h_s, *, cw, rw, chunk):
    tm = x_ref.shape[1]
    gain = gain_ref[...]
    xh = jnp.where(pl.program_id(1) > 0, xh_ref[0], 0.0)
    h_s[0:HALO, :] = _rms(xh, gain).astype(BF16)
    h_s[HALO:, :] = _rms(x_ref[0], gain).astype(BF16)

    def proj(c0, n):
        return _dot(h_s[...], w_ref[:, c0:c0 + n])

    def earlier(val, n):
        return pltpu.roll(val, n, axis=0)[HALO:]

    gate_b = proj(0, cw)[HALO:]
    u = proj(cw, cw) * proj(2 * cw, cw)
    cwt = convw_ref[...]
    yc_o[0] = (gate_b * (earlier(u, 2) * cwt[0:1] + earlier(u, 1) * cwt[1:2] + u[HALO:] * cwt[2:3])).astype(BF16)

    def shifted(c0, n):
        p = proj(c0, n)
        cur = p[HALO:]
        return cur + (earlier(p, 1) - cur) * mu_ref[:, c0 - 3 * cw:c0 - 3 * cw + n]

    base = 3 * cw
    r = shifted(base, rw)
    k = shifted(base + rw, rw)
    v = shifted(base + 2 * rw, rw)
    lora = shifted(base + 3 * rw, 3 * LANES)
    dw, da, dg = lora[:, 0:LANES], lora[:, LANES:2 * LANES], lora[:, 2 * LANES:3 * LANES]

    zw = w0_ref[...] + _dot_hi_lo(jnp.tanh(dw), wup_ref)
    lw = -math.exp(-0.5) * jax.nn.sigmoid(zw)
    iclr = jax.nn.sigmoid(a0_ref[...] + _dot_bf(da, aup_ref[0]))
    g_o[0] = _dot_bf(jax.nn.sigmoid(dg), gup_ref[0]).astype(BF16)
    hsum = hsum_ref[...]
    kkr = k * kk_ref[...]
    kk = kkr * lax.rsqrt(jnp.maximum(_dot_bf(kkr * kkr, hsum), KK_NORM_FLOOR))
    kh = k * (1.0 + (iclr - 1.0) * ka_ref[...])
    bonus_o[0] = (_dot_bf(r * kh * rk_ref[...], hsum) * v).astype(BF16)
    b_vec = kk * iclr
    v_o[0] = v.astype(BF16)

    tri = tri_ref[...]
    for c in range(tm // chunk):
        rows = slice(c * chunk, (c + 1) * chunk)
        lw_c = lw[rows]
        cum = _dot_split_rhs(tri, lw_c, 2)
        tot = cum[chunk - 1:chunk, :]
        grow = jnp.exp(-cum)
        to_end = jnp.exp(tot - cum)
        a_o[0, rows, :] = (-kk[rows] * jnp.exp(cum - lw_c)).astype(BF16)
        r_o[0, rows, :] = (r[rows] * jnp.exp(cum)).astype(BF16)
        b_o[0, rows, :] = (b_vec[rows] * grow).astype(BF16)
        k_o[0, rows, :] = (kh[rows] * grow).astype(BF16)
        bw_o[0, rows, :] = (b_vec[rows] * to_end).astype(BF16)
        kw_o[0, rows, :] = (kh[rows] * to_end).astype(BF16)
        wend_o[0, c] = jnp.exp(tot)


def _rwkv_chunk_kernel(a_ref, r_ref, b_ref, k_ref, v_ref, bw_ref, kw_ref, wend_ref, y_ref, st_ref, *, chunk):
    @pl.when(pl.program_id(1) == 0)
    def _():
        st_ref[...] = jnp.zeros_like(st_ref)

    tc, width = a_ref.shape[1], a_ref.shape[2]
    c2 = 2 * chunk
    assert c2 == LANES
    n_pairs = width // LANES
    units = [(c, p) for c in range(tc // chunk) for p in range(n_pairs)]
    row = lax.broadcasted_iota(jnp.int32, (c2, c2), 0)
    col = lax.broadcasted_iota(jnp.int32, (c2, c2), 1)
    same = (row // chunk) == (col // chunk)
    strict = same & (col < row)
    incl = same & (col <= row)
    eye = row == col
    head0 = lax.broadcasted_iota(jnp.int32, (chunk, LANES), 1) < chunk

    def stacked(ref):
        out = []
        for c, p in units:
            x = ref[0, c * chunk:(c + 1) * chunk, p * LANES:(p + 1) * LANES]
            zero = jnp.zeros_like(x)
            out.append(jnp.concatenate([jnp.where(head0, x, zero), jnp.where(head0, zero, x)], axis=0))
        return out

    a_s, r_s, b_s, k_s, v_s, bw_s, kw_s = (stacked(ref) for ref in (a_ref, r_ref, b_ref, k_ref, v_ref, bw_ref, kw_ref))
    each = lambda fn: [fn(u) for u in range(len(units))]
    att = each(lambda u: _dot_nt(jnp.concatenate([a_s[u], r_s[u]], axis=0), jnp.concatenate([b_s[u], k_s[u]], axis=0)))
    power = each(lambda u: jnp.where(strict, att[u][:c2, :c2], 0.0).astype(BF16))
    a_ak = each(lambda u: jnp.where(strict, att[u][:c2, c2:], 0.0).astype(BF16))
    a_rb = each(lambda u: jnp.where(incl, att[u][c2:, :c2], 0.0).astype(BF16))
    a_rk = each(lambda u: jnp.where(incl, att[u][c2:, c2:], 0.0).astype(BF16))
    av = each(lambda u: _dot(a_ak[u], v_s[u]).astype(BF16))
    tinv = each(lambda u: jnp.where(eye, 1.0, 0.0) + power[u].astype(F32))
    for _ in range(int(math.log2(chunk)) - 1):
        power = each(lambda u: _dot(power[u], power[u]).astype(BF16))
        tinv = each(lambda u: tinv[u] + _dot(tinv[u].astype(BF16), power[u]))
    pq = each(lambda u: _dot(tinv[u].astype(BF16), jnp.concatenate([a_s[u], av[u]], axis=1)).astype(BF16))
    zero = jnp.zeros((c2, c2), BF16)
    pqv = each(lambda u: jnp.concatenate([pq[u], jnp.concatenate([zero, v_s[u]], axis=1)], axis=0))
    gy = each(lambda u: _dot(jnp.concatenate([a_rb[u], a_rk[u]], axis=1), pqv[u]))
    mn = each(lambda u: _dot_tn(jnp.concatenate([bw_s[u], kw_s[u]], axis=0), pqv[u]))
    g_mat = each(lambda u: (r_s[u].astype(F32) + gy[u][:, :c2]).astype(BF16))
    y0 = each(lambda u: gy[u][:, c2:])
    n0 = each(lambda u: mn[u][:, c2:])

    def decay_plus(u):
        c, p = units[u]
        w_end = wend_ref[0, c, :, p * LANES:(p + 1) * LANES]
        return (jnp.where(eye, w_end, 0.0) + mn[u][:, :c2]).astype(BF16)

    m_mat = each(decay_plus)
    state = [st_ref[p] for p in range(n_pairs)]
    for u, (c, p) in enumerate(units):
        s_bf = state[p].astype(BF16)
        ys = _dot(g_mat[u], s_bf) + y0[u]
        state[p] = _dot(m_mat[u], s_bf) + n0[u]
        y_ref[0, c * chunk:(c + 1) * chunk, p * LANES:(p + 1) * LANES] = (ys[:chunk] + ys[chunk:]).astype(y_ref.dtype)
    for p in range(n_pairs):
        st_ref[p] = state[p]


def _even_out_mix(y_ref, bonus_ref, g_ref, yc_ref, lnw_ref, lnb_ref, hmean_ref, wo_ref, *, cw):
    y = y_ref[...].astype(F32)
    hmean = hmean_ref[...]
    d = y - _dot_bf(y, hmean)
    var = _dot_bf(d * d, hmean)
    yn = d * lax.rsqrt(var + RWKV_GN_EPS) * lnw_ref[...] + lnb_ref[...]
    y_rwkv = (yn + bonus_ref[...].astype(F32)) * g_ref[...].astype(F32)
    return _dot(yc_ref[...], wo_ref[0:cw, :]) + _dot(y_rwkv.astype(BF16), wo_ref[cw:, :])


def _conv_rwkv_mixer(x, gain, w_in, conv_w, mu_shift, w0, w_up, a0, a_up, g_up, k_k, k_a, r_k, ln_w, ln_b, w_out):
    b, t, d = x.shape
    cw = conv_w.shape[1]
    heads, hd = r_k.shape
    rw = heads * hd
    ranks = (w_up.shape[0], a_up.shape[0], g_up.shape[0])
    assert max(ranks) <= LANES and rw % LANES == 0 and cw == rw and LANES % hd == 0

    base = 3 * cw + 3 * rw
    offs = (base, base + ranks[0], base + ranks[0] + ranks[1])

    def padded_cols(arr):
        blocks = [jnp.pad(arr[..., o:o + n], [(0, 0)] * (arr.ndim - 1) + [(0, LANES - n)])
                  for o, n in zip(offs, ranks)]
        return jnp.concatenate([arr[..., :base]] + blocks, axis=-1)

    w_cat = padded_cols(w_in).astype(BF16)
    mu_cat = padded_cols(jnp.pad(mu_shift.reshape(1, -1), ((0, 0), (3 * cw, 0))))[:, 3 * cw:]

    def pad_rows(m):
        m = jnp.pad(m, ((0, LANES - m.shape[0]), (0, 0)))
        hi = m.astype(BF16)
        return jnp.stack([hi, (m - hi.astype(F32)).astype(BF16)])

    head_id = np.arange(rw) // hd
    hsum = jnp.asarray((head_id[:, None] == head_id[None, :]).astype(np.float32)).astype(BF16)
    tri = jnp.asarray(np.tril(np.ones((RWKV_CHUNK, RWKV_CHUNK), np.float32))).astype(BF16)
    row = lambda vec: vec.reshape(1, -1)

    tm = _tile(t, TOKEN_TILE)
    wide = w_cat.shape[1]
    assert tm % RWKV_CHUNK == 0
    chunks = tm // RWKV_CHUNK
    tok = pl.BlockSpec((1, tm, rw), lambda bi, i: (bi, i, 0))
    seq_bf = jax.ShapeDtypeStruct((b, t, rw), BF16)
    a_sc, r_sc, b_sc, k_sc, v_bf, b_end, k_end, w_end, gate, y_conv, bonus = pl.pallas_call(
        functools.partial(_even_in_kernel, cw=cw, rw=rw, chunk=RWKV_CHUNK),
        out_shape=[seq_bf] * 7 + [jax.ShapeDtypeStruct((b, t // RWKV_CHUNK, 1, rw), F32)] + [seq_bf] * 3,
        grid=(b, t // tm),
        in_specs=[pl.BlockSpec((1, tm, d), lambda bi, i: (bi, i, 0)),
                  pl.BlockSpec((1, HALO, d), lambda bi, i: (bi, jnp.maximum(i * (tm // HALO) - 1, 0), 0)),
                  _resident((1, d)), _resident((d, wide)), _resident((1, wide - 3 * cw)),
                  _resident(conv_w.shape), _resident((1, rw)), _resident((2, LANES, rw)),
                  _resident((1, rw)), _resident((2, LANES, rw)), _resident((2, LANES, rw)),
                  _resident((1, rw)), _resident((1, rw)), _resident((1, rw)), _resident((rw, rw)),
                  _resident((RWKV_CHUNK, RWKV_CHUNK))],
        out_specs=[tok] * 7 + [pl.BlockSpec((1, chunks, 1, rw), lambda bi, i: (bi, i, 0, 0))] + [tok] * 3,
        scratch_shapes=[pltpu.VMEM((tm + HALO, d), BF16)],
        compiler_params=_params("parallel", "parallel"),
        name="even_in",
    )(x, x, row(gain), w_cat, mu_cat, conv_w, row(w0), pad_rows(w_up), row(a0), pad_rows(a_up),
      pad_rows(g_up), row(k_k), row(k_a), row(r_k), hsum, tri)

    tc = _tile(t, RWKV_BLOCK)
    blk = pl.BlockSpec((1, tc, rw), lambda bi, j: (bi, j, 0))
    y = pl.pallas_call(
        functools.partial(_rwkv_chunk_kernel, chunk=RWKV_CHUNK),
        out_shape=seq_bf,
        grid=(b, t // tc),
        in_specs=[blk] * 7 + [pl.BlockSpec((1, tc // RWKV_CHUNK, 1, rw), lambda bi, j: (bi, j, 0, 0))],
        out_specs=blk,
        scratch_shapes=[pltpu.VMEM((rw // LANES, LANES, LANES), F32)],
        compiler_params=_params("parallel", "arbitrary"),
        name="rwkv_chunk",
    )(a_sc, r_sc, b_sc, k_sc, v_bf, b_end, k_end, w_end)

    m = b * t
    tmo = _tile(m, TOKEN_TILE)
    flat = lambda z: z.reshape(m, -1)
    tokf = pl.BlockSpec((tmo, rw), lambda i: (i, 0))
    operands = [flat(y), flat(bonus), flat(gate), flat(y_conv), row(ln_w), row(ln_b),
                hsum / hd,
                w_out.astype(BF16)]
    specs = [tokf, tokf, tokf, tokf, _resident((1, rw)), _resident((1, rw)), _resident((rw, rw)),
             _resident((cw + rw, d))]
    return functools.partial(_even_out_mix, cw=cw), operands, specs


def _rope_table_kernel(pos_ref, freq_ref, sel_ref, cs_ref):
    ang = pos_ref[...].astype(F32) * freq_ref[...]
    cos, sin = _split(jnp.cos(ang), 3), _split(jnp.sin(ang), 3)
    tr = ang.shape[0]
    for g in range(sel_ref.shape[0]):
        sel = sel_ref[g]
        cs_ref[g * tr:(g + 1) * tr, 0:LANES] = sum(_dot(piece, sel) for piece in cos)
        cs_ref[g * tr:(g + 1) * tr, LANES:2 * LANES] = sum(_dot(piece, sel) for piece in sin)


def _rope_tables(positions, rope_dim):
    b, t = positions.shape
    half = rope_dim // 2
    assert LANES % half == 0
    per_row = LANES // half
    tr = _tile(b * t // per_row, 256)
    tiles = b * t // (per_row * tr)
    inv_freq = ROPE_THETA ** (-jnp.arange(0, rope_dim, 2, dtype=F32) / rope_dim)
    pos_rep = jnp.repeat(positions.reshape(tiles, per_row, tr).transpose(0, 2, 1), half, axis=2).reshape(tiles * tr, LANES)
    lane = np.arange(LANES)
    sel = jnp.asarray(np.stack([(lane[:, None] == g * half + lane[None, :] % half) for g in range(per_row)])
                      .astype(np.float32)).astype(BF16)
    cs = pl.pallas_call(
        _rope_table_kernel,
        out_shape=jax.ShapeDtypeStruct((b * t, 2 * LANES), F32),
        grid=(tiles,),
        in_specs=[pl.BlockSpec((tr, LANES), lambda i: (i, 0)), _resident((1, LANES)), _resident(sel.shape)],
        out_specs=pl.BlockSpec((per_row * tr, 2 * LANES), lambda i: (i, 0)),
        compiler_params=_params("parallel"),
        name="rope_table",
    )(pos_rep, jnp.tile(inv_freq, per_row).reshape(1, LANES), sel)
    return cs.reshape(b, t, 2 * LANES)


def _mla_in_kernel(x_ref, cs_ref, gain_ref, win_ref, qa_ref, kva_ref, wq_ref, wkn_ref, wvt_ref,
                   gqn_ref, gqa_ref, gqb_ref, gkn_ref, gka_ref, gkb_ref, ones_ref, q_o, k_o, v_o,
                   *, heads, q_rank, kv_rank, nope, rope, scale):
    qk_dim = nope + rope
    tm = x_ref.shape[1]
    h = _rms(x_ref[0], gain_ref[...]).astype(BF16)
    p = _dot(h, win_ref[...])
    c_q = _rms(p[:, :q_rank], qa_ref[...]).astype(BF16)
    c_kv = _rms(p[:, q_rank:q_rank + kv_rank], kva_ref[...]).astype(BF16)
    base = q_rank + kv_rank
    k_a, k_b = p[:, base:base + LANES], p[:, base + LANES:base + 2 * LANES]
    q = _dot(c_q, wq_ref[...])
    kn = _dot(c_kv, wkn_ref[...])
    cos, sin = cs_ref[0, :, 0:LANES], cs_ref[0, :, LANES:2 * LANES]
    lane = lax.broadcasted_iota(jnp.int32, (tm, LANES), 1)
    first = lane < rope
    ones = ones_ref[...]
    pairs = heads // 2
    rope_a = [q[:, (heads + j) * LANES:(heads + j + 1) * LANES] for j in range(pairs)]
    rope_b = [q[:, (heads + pairs + j) * LANES:(heads + pairs + j + 1) * LANES] for j in range(pairs)]
    q_rot = [rope_a[j] * gqa_ref[...] * cos + rope_b[j] * gqb_ref[...] * sin for j in range(pairs)]
    k_rot = k_a * gka_ref[...] * cos + k_b * gkb_ref[...] * sin
    k_rot_sq = jnp.where(first, k_a * k_a, 0.0)
    mine = [first if hd % 2 == 0 else jnp.logical_not(first) for hd in range(heads)]
    q_n = [q[:, hd * LANES:(hd + 1) * LANES] for hd in range(heads)]
    k_n = [kn[:, hd * nope:(hd + 1) * nope] for hd in range(heads)]
    sq_q = [q_n[hd] * q_n[hd] + jnp.where(mine[hd], rope_a[hd // 2] * rope_a[hd // 2], 0.0) for hd in range(heads)]
    sq_k = [k_n[hd] * k_n[hd] + k_rot_sq for hd in range(heads)]
    pair_sum = lambda sq, j: _dot(jnp.concatenate([sq[2 * j], sq[2 * j + 1]], axis=1).astype(BF16), ones)
    ss_q = [pair_sum(sq_q, j) for j in range(pairs)]
    ss_k = [pair_sum(sq_k, j) for j in range(pairs)]
    for hd in range(heads):
        half = slice((hd % 2) * LANES, (hd % 2 + 1) * LANES)
        rs_q = lax.rsqrt(ss_q[hd // 2][:, half] / qk_dim + NORM_EPS) * scale
        rs_k = lax.rsqrt(ss_k[hd // 2][:, half] / qk_dim + NORM_EPS)
        q_o[0, hd, :, 0:nope] = (q_n[hd] * rs_q * gqn_ref[...]).astype(BF16)
        q_o[0, hd, :, nope:nope + LANES] = (jnp.where(mine[hd], q_rot[hd // 2], 0.0) * rs_q).astype(BF16)
        k_o[0, hd, :, 0:nope] = (k_n[hd] * rs_k * gkn_ref[...]).astype(BF16)
        k_o[0, hd, :, nope:nope + LANES] = (k_rot * rs_k).astype(BF16)
        v_o[0, hd] = _dot_nt(wvt_ref[hd], c_kv).astype(BF16)


def _flash_kernel(q_ref, k_ref, v_ref, o_ref, sa_ref, sb_ref, m_ref, l_ref, acc_ref):
    i = pl.program_id(2)
    nh, tq = q_ref.shape[1], q_ref.shape[2]
    vd = o_ref.shape[2] // nh
    heads = range(nh)

    def block(j):
        return pl.ds(pl.multiple_of(j * tq, tq), tq)

    def logits(h, j):
        return _dot_nt(k_ref[0, h, block(j), :], q_ref[0, h])

    def consume(h, j, s):
        m = m_ref[h]
        m_new = jnp.maximum(m, jnp.max(s, axis=0, keepdims=True))
        alpha = jnp.exp2(m - m_new)
        pr = jnp.exp2(s - m_new)
        l_ref[h] = alpha * l_ref[h] + jnp.sum(pr, axis=0, keepdims=True)
        acc_ref[h] = alpha * acc_ref[h] + _dot(v_ref[0, h, :, block(j)], pr.astype(BF16))
        m_ref[h] = m_new

    def causal(s):
        keep = lax.broadcasted_iota(jnp.int32, s.shape, 0) <= lax.broadcasted_iota(jnp.int32, s.shape, 1)
        return jnp.where(keep, s, -jnp.inf)

    m_ref[...] = jnp.full(m_ref.shape, -jnp.inf, F32)
    l_ref[...] = jnp.zeros(l_ref.shape, F32)
    acc_ref[...] = jnp.zeros(acc_ref.shape, F32)
    for h in heads:
        sa_ref[h] = logits(h, 0)

    def pair(t, carry):
        j = 2 * t
        for h in heads:
            sb_ref[h] = logits(h, j + 1)
        for h in heads:
            consume(h, j, sa_ref[h])
        for h in heads:
            sa_ref[h] = logits(h, j + 2)
        for h in heads:
            consume(h, j + 1, sb_ref[h])
        return carry

    lax.fori_loop(0, i // 2, pair, 0)

    @pl.when(i % 2 == 0)
    def _():
        for h in heads:
            consume(h, i, causal(sa_ref[h]))

    @pl.when(i % 2 == 1)
    def _():
        for h in heads:
            sb_ref[h] = logits(h, i)
        for h in heads:
            consume(h, i - 1, sa_ref[h])
        for h in heads:
            consume(h, i, causal(sb_ref[h]))

    for h in heads:
        o_ref[0, :, h * vd:(h + 1) * vd] = (acc_ref[h] / l_ref[h]).T.astype(o_ref.dtype)


def _mla_out_mix(o_ref, wo_ref):
    return _dot(o_ref[...], wo_ref[...])


def _mla_mixer(x, cs, gain, w_in, q_a_norm, kv_a_norm, w_q_up, w_kv_up, q_norm, k_norm, w_out):
    b, t, d = x.shape
    q_rank, kv_rank, qk_dim = q_a_norm.shape[0], kv_a_norm.shape[0], q_norm.shape[0]
    rope = w_in.shape[1] - q_rank - kv_rank
    nope = qk_dim - rope
    heads = w_q_up.shape[1] // qk_dim
    v_dim = w_kv_up.shape[1] // heads - nope
    half = rope // 2
    blk = nope + LANES
    assert nope == LANES and v_dim == LANES and 2 * rope == LANES and heads % 2 == 0
    assert q_rank % LANES == 0 and kv_rank % LANES == 0

    def rot_cols(w):
        return jnp.concatenate([-w[..., half:], w[..., :half]], axis=-1)

    def rot_gain(g):
        return jnp.concatenate([g[half:], g[:half]])

    twice = lambda z: jnp.concatenate([z, z], axis=-1)
    w_pe = w_in[:, q_rank + kv_rank:]
    w_in_ext = jnp.concatenate([w_in[:, :q_rank + kv_rank], twice(w_pe), twice(rot_cols(w_pe))], axis=1).astype(BF16)
    wq = w_q_up.reshape(q_rank, heads, qk_dim)
    wq_ext = jnp.concatenate([wq[..., :nope].reshape(q_rank, heads * nope),
                              wq[..., nope:].reshape(q_rank, heads * rope),
                              rot_cols(wq[..., nope:]).reshape(q_rank, heads * rope)], axis=1).astype(BF16)
    wkv = w_kv_up.reshape(kv_rank, heads, nope + v_dim)
    w_kn = wkv[..., :nope].reshape(kv_rank, heads * nope).astype(BF16)
    w_vt = wkv[..., nope:].transpose(1, 2, 0).astype(BF16)
    row = lambda vec: vec.reshape(1, -1)
    lane_block = np.arange(2 * LANES) // LANES
    ones = jnp.asarray((lane_block[:, None] == lane_block[None, :]).astype(np.float32)).astype(BF16)

    tm = _tile(t, TOKEN_TILE)
    head_blk = pl.BlockSpec((1, heads, tm, blk), lambda bi, i: (bi, 0, i, 0))
    head_arr = jax.ShapeDtypeStruct((b, heads, t, blk), BF16)
    lane_row = _resident((1, LANES))
    q, k, v = pl.pallas_call(
        functools.partial(_mla_in_kernel, heads=heads, q_rank=q_rank, kv_rank=kv_rank, nope=nope, rope=rope,
                          scale=qk_dim ** -0.5 * math.log2(math.e)),
        out_shape=[head_arr, head_arr, jax.ShapeDtypeStruct((b, heads, v_dim, t), BF16)],
        grid=(b, t // tm),
        in_specs=[pl.BlockSpec((1, tm, d), lambda bi, i: (bi, i, 0)),
                  pl.BlockSpec((1, tm, 2 * LANES), lambda bi, i: (bi, i, 0)),
                  _resident((1, d)), _resident(w_in_ext.shape), _resident((1, q_rank)), _resident((1, kv_rank)),
                  _resident(wq_ext.shape), _resident(w_kn.shape), _resident(w_vt.shape),
                  lane_row, lane_row, lane_row, lane_row, lane_row, lane_row, _resident(ones.shape)],
        out_specs=[head_blk, head_blk, pl.BlockSpec((1, heads, v_dim, tm), lambda bi, i: (bi, 0, 0, i))],
        compiler_params=_params("parallel", "parallel"),
        name="mla_in",
    )(x, cs, row(gain), w_in_ext, row(q_a_norm), row(kv_a_norm), wq_ext, w_kn, w_vt,
      row(q_norm[:nope]), row(twice(q_norm[nope:])), row(twice(rot_gain(q_norm[nope:]))),
      row(k_norm[:nope]), row(twice(k_norm[nope:])), row(twice(rot_gain(k_norm[nope:]))), ones)

    tq = _tile(t, ATTN_TILE)
    nh = ATTN_HEADS_PER_STEP
    assert heads % nh == 0
    full = pl.BlockSpec((1, nh, t, blk), lambda bi, hi, i: (bi, hi, 0, 0))
    o = pl.pallas_call(
        _flash_kernel,
        out_shape=jax.ShapeDtypeStruct((b, t, heads * v_dim), BF16),
        grid=(b, heads // nh, t // tq),
        in_specs=[pl.BlockSpec((1, nh, tq, blk), lambda bi, hi, i: (bi, hi, i, 0)), full,
                  pl.BlockSpec((1, nh, v_dim, t), lambda bi, hi, i: (bi, hi, 0, 0))],
        out_specs=pl.BlockSpec((1, tq, nh * v_dim), lambda bi, hi, i: (bi, i, hi)),
        scratch_shapes=[pltpu.VMEM((nh, tq, tq), F32), pltpu.VMEM((nh, tq, tq), F32), pltpu.VMEM((nh, 1, tq), F32),
                        pltpu.VMEM((nh, 1, tq), F32), pltpu.VMEM((nh, v_dim, tq), F32)],
        compiler_params=_params("parallel", "parallel", "arbitrary"),
        name="flash",
    )(q, k, v)

    m = b * t
    tmo = _tile(m, TOKEN_TILE)
    operands = [o.reshape(m, heads * v_dim), w_out.astype(BF16)]
    specs = [pl.BlockSpec((tmo, heads * v_dim), lambda i: (i, 0)), _resident(w_out.shape)]
    return _mla_out_mix, operands, specs


def kernel(x, positions, norm_gains, ffn_w_gate, ffn_w_up, ffn_w_down, even_w_in, even_conv_w, even_mu_shift, rwkv_w0, rwkv_w_up, rwkv_a0, rwkv_a_up, rwkv_g_up, rwkv_k_k, rwkv_k_a, rwkv_r_k, rwkv_ln_w, rwkv_ln_b, even_w_out, odd_w_in, mla_q_a_norm, mla_kv_a_norm, mla_w_q_up, mla_w_kv_up, mla_q_norm, mla_k_norm, odd_w_out):
    b, t, d = x.shape
    depth = norm_gains.shape[0]
    rope_dim = odd_w_in.shape[2] - mla_q_a_norm.shape[1] - mla_kv_a_norm.shape[1]
    cs = _rope_tables(positions, rope_dim)

    w_gate, w_up, w_down = ffn_w_gate.astype(BF16), ffn_w_up.astype(BF16), ffn_w_down.astype(BF16)

    def ffn(x, layer, which, gain, mix=None):
        return _ffn(x.reshape(b * t, d), gain, w_gate, w_up, w_down, (layer, which), mix).reshape(b, t, d)

    for layer in range(depth):
        g = norm_gains[layer]
        x = ffn(x, layer, 0, g[0])
        if layer % 2 == 0:
            i = layer // 2
            mix = _conv_rwkv_mixer(x, g[1], even_w_in[i], even_conv_w[i], even_mu_shift[i], rwkv_w0[i],
                                   rwkv_w_up[i], rwkv_a0[i], rwkv_a_up[i], rwkv_g_up[i], rwkv_k_k[i],
                                   rwkv_k_a[i], rwkv_r_k[i], rwkv_ln_w[i], rwkv_ln_b[i], even_w_out[i])
        else:
            j = layer // 2
            mix = _mla_mixer(x, cs, g[1], odd_w_in[j], mla_q_a_norm[j], mla_kv_a_norm[j], mla_w_q_up[j],
                             mla_w_kv_up[j], mla_q_norm[j], mla_k_norm[j], odd_w_out[j])
        x = ffn(x, layer, 1, g[2], mix)
    return x
```
